```python
import functools
import jax, jax.numpy as jnp
from jax import lax
import numpy as np

D_MODEL = 1024
BATCH = 8
SEQ = 2048
DEPTH = 4
DEC_BATCH = 128
DEC_SEQ = 8
PAST_LEN = 2048
PAGE_SIZE = 128

A_HEADS = 8
A_HEAD_DIM = D_MODEL // 16
A_WIDTH = A_HEADS * A_HEAD_DIM
DIL_GROUPS = ((128, 1), (512, 4), (2048, 16))
WIN_MAX = max(w for w, _ in DIL_GROUPS)
ATT_BLK = 128
B_HEADS = 4
B_KEY_DIM = D_MODEL // 16
B_VAL_DIM = D_MODEL // 16
B_WIDTH = B_HEADS * B_VAL_DIM
RET_CHUNK = 128
C_WIDTH = D_MODEL // 4
CONV_W = 31

MIX_WIDTH = A_WIDTH + B_WIDTH + C_WIDTH
IN_SPLITS = (A_WIDTH, A_WIDTH, A_WIDTH, A_WIDTH,
             B_HEADS * B_KEY_DIM, B_HEADS * B_KEY_DIM, B_WIDTH, B_WIDTH,
             C_WIDTH, C_WIDTH, C_WIDTH)
IN_WIDTH = sum(IN_SPLITS)
IN_OFFSETS = tuple(int(c) for c in np.cumsum(IN_SPLITS)[:-1])
EPS = 1e-6
NEG = -1e30

kernel_name = "hymba_dilated_retention_conformer_step"


def _rmsnorm(x, g):
    xf = x.astype(jnp.float32)
    y = xf * lax.rsqrt(jnp.mean(xf * xf, axis=-1, keepdims=True) + EPS) * g.astype(jnp.float32)
    return y.astype(x.dtype)


def _alibi_slopes(n_heads):
    return jnp.exp2(-8.0 * (jnp.arange(n_heads, dtype=jnp.float32) + 1.0) / n_heads)


def _combine(outs, lses):
    wts = jax.nn.softmax(jnp.stack(lses, axis=0), axis=0)
    return jnp.sum(wts[..., None] * jnp.stack(outs, axis=0), axis=0)


def _band_attn(q, k, v, slopes_r, n_back):
    n, L, H, D = q.shape
    nb = L // ATT_BLK

    def blocks(t):
        return t.reshape(n, nb, ATT_BLK, H, D)

    def with_prev(t):
        tb = blocks(t)
        prev = jnp.pad(tb, ((0, 0), (1, 0), (0, 0), (0, 0), (0, 0)))[:, :-1]
        return jnp.concatenate([prev, tb], axis=2)

    qb, k2, v2 = blocks(q), with_prev(k), with_prev(v)
    qi = jnp.arange(ATT_BLK)[:, None]
    ki = jnp.arange(2 * ATT_BLK)[None, :] - ATT_BLK
    dist = qi - ki
    key_pos = jnp.arange(nb)[:, None, None] * ATT_BLK + ki[None]
    valid = (dist >= 0) & (dist <= n_back) & (key_pos >= 0)
    s = jnp.einsum('nbqhd,nbkhd->nbhqk', qb, k2).astype(jnp.float32) * (A_HEAD_DIM ** -0.5)
    s = s - slopes_r[:, None, None] * dist.astype(jnp.float32)
    s = jnp.where(valid[None, :, None], s, NEG)
    m = jnp.max(s, axis=-1, keepdims=True)
    p = jnp.exp(s - m)
    den = jnp.sum(p, axis=-1)
    o = jnp.einsum('nbhqk,nbkhd->nbqhd', p, v2.astype(jnp.float32)) / jnp.swapaxes(den, 2, 3)[..., None]
    lse = m[..., 0] + jnp.log(den)
    return o.reshape(n, L, H, D), jnp.swapaxes(lse, 2, 3).reshape(n, L, H)


def _dilated_attn_prompt(q, k, v):
    B, S, H, D = q.shape
    slopes = _alibi_slopes(H)
    outs, lses = [], []
    for w, r in DIL_GROUPS:
        n_back = w // r
        ls = S // r
        lp = -(-ls // ATT_BLK) * ATT_BLK

        def sub(t):
            t = t.reshape(B, ls, r, H, D).transpose(0, 2, 1, 3, 4).reshape(B * r, ls, H, D)
            return jnp.pad(t, ((0, 0), (0, lp - ls), (0, 0), (0, 0)))

        o, lse = _band_attn(sub(q), sub(k), sub(v), slopes * r, n_back)
        outs.append(o[:, :ls].reshape(B, r, ls, H, D).transpose(0, 2, 1, 3, 4).reshape(B, S, H, D))
        lses.append(lse[:, :ls].reshape(B, r, ls, H).transpose(0, 2, 1, 3).reshape(B, S, H))
    return _combine(outs, lses).astype(q.dtype)


def _dilated_attn_sample(q, k_new, v_new, k_buf, v_buf):
    Bd, T, H, D = q.shape
    l_buf = k_buf.shape[1]
    k_all = jnp.concatenate([k_buf.astype(k_new.dtype), k_new], axis=1)
    v_all = jnp.concatenate([v_buf.astype(v_new.dtype), v_new], axis=1)
    slopes = _alibi_slopes(H)
    outs, lses = [], []
    for w, r in DIL_GROUPS:
        n_back = w // r
        j = jnp.arange(n_back + 1)
        idx = l_buf + jnp.arange(T)[:, None] - r * j[None, :]
        valid = idx >= 0
        idx = jnp.maximum(idx, 0)
        kg = k_all[:, idx]
        vg = v_all[:, idx]
        s = jnp.einsum('bthd,btjhd->bhtj', q, kg).astype(jnp.float32) * (A_HEAD_DIM ** -0.5)
        s = s - (slopes * r)[:, None, None] * j.astype(jnp.float32)
        s = jnp.where(valid[None, None], s, NEG)
        m = jnp.max(s, axis=-1, keepdims=True)
        p = jnp.exp(s - m)
        den = jnp.sum(p, axis=-1)
        o = jnp.einsum('bhtj,btjhd->bthd', p, vg.astype(jnp.float32)) / jnp.swapaxes(den, 1, 2)[..., None]
        outs.append(o)
        lses.append(jnp.swapaxes(m[..., 0] + jnp.log(den), 1, 2))
    return _combine(outs, lses).astype(q.dtype)


def _retention(q, k, v, s0, chunk, gn_g):
    Bn, L, H, DK = q.shape
    DV = v.shape[-1]
    nc = L // chunk
    f32 = jnp.float32
    log_g = jnp.log(1.0 - jnp.exp2(-5.0 - jnp.arange(H, dtype=f32)))
    i = jnp.arange(chunk, dtype=f32)
    diff = i[:, None] - i[None, :]
    dmat = jnp.where(diff >= 0, jnp.exp(jnp.maximum(diff, 0.0)[None] * log_g[:, None, None]), 0.0)
    q_dec = jnp.exp((i[:, None] + 1.0) * log_g[None, :])
    k_dec = jnp.exp((chunk - 1.0 - i)[:, None] * log_g[None, :])
    c_dec = jnp.exp(chunk * log_g)

    def chunks(t):
        return jnp.swapaxes(t.astype(f32).reshape(Bn, nc, chunk, H, t.shape[-1]), 0, 1)

    qc, kc, vc = chunks(q), chunks(k * (DK ** -0.5)), chunks(v)

    def step(S, inp):
        qi, ki, vi = inp
        att = jnp.einsum('bihd,bjhd->bhij', qi, ki) * dmat
        o = (jnp.einsum('bhij,bjhv->bihv', att, vi)
             + jnp.einsum('bihd,bhdv->bihv', qi, S) * q_dec[None, :, :, None])
        S = S * c_dec[None, :, None, None] + jnp.einsum('bjhd,bjhv->bhdv', ki * k_dec[None, :, :, None], vi)
        return S, o

    s_fin, o = lax.scan(step, s0.astype(f32), (qc, kc, vc))
    o = jnp.swapaxes(o, 0, 1).reshape(Bn, L, H, DV)
    mu = jnp.mean(o, axis=-1, keepdims=True)
    var = jnp.mean(jnp.square(o - mu), axis=-1, keepdims=True)
    o = ((o - mu) * lax.rsqrt(var + EPS)).reshape(Bn, L, H * DV) * gn_g.astype(f32)
    return o.astype(q.dtype), s_fin


def _conformer_conv(ua, ub, s0, dw_w, dw_b, ln_g, ln_b, pw_w, pw_b):
    u = ua * jax.nn.sigmoid(ub)
    u_ext = jnp.concatenate([s0.astype(u.dtype), u], axis=1)
    y = lax.conv_general_dilated(u_ext, dw_w[:, None, :].astype(u.dtype), (1,), 'VALID',
                                 dimension_numbers=('NWC', 'WIO', 'NWC'),
                                 feature_group_count=C_WIDTH) + dw_b
    yf = y.astype(jnp.float32)
    mu = jnp.mean(yf, axis=-1, keepdims=True)
    var = jnp.mean(jnp.square(yf - mu), axis=-1, keepdims=True)
    yn = (yf - mu) * lax.rsqrt(var + EPS) * ln_g.astype(jnp.float32) + ln_b.astype(jnp.float32)
    yn = jax.nn.silu(yn).astype(u.dtype)
    return yn @ pw_w + pw_b, u_ext[:, u_ext.shape[1] - (CONV_W - 1):]


def _layer(x, p, attn_fn, ret_s0, ret_chunk, conv_s0):
    g, w_i, w_o, ret_g, dw_w, dw_b, ln_g, ln_b, pw_w, pw_b = p
    Bn, L, _ = x.shape
    h = _rmsnorm(x, g)
    z = h @ w_i
    qa, ka, va, ga, qb, kb, vb, gb, ua, ub, gc = jnp.split(z, IN_OFFSETS, axis=-1)
    ka_h = ka.reshape(Bn, L, A_HEADS, A_HEAD_DIM)
    va_h = va.reshape(Bn, L, A_HEADS, A_HEAD_DIM)
    ya = attn_fn(qa.reshape(Bn, L, A_HEADS, A_HEAD_DIM), ka_h, va_h).reshape(Bn, L, A_WIDTH)
    yb, ret_s = _retention(qb.reshape(Bn, L, B_HEADS, B_KEY_DIM), kb.reshape(Bn, L, B_HEADS, B_KEY_DIM),
                           vb.reshape(Bn, L, B_HEADS, B_VAL_DIM), ret_s0, ret_chunk, ret_g)
    yc, conv_s = _conformer_conv(ua, ub, conv_s0, dw_w, dw_b, ln_g, ln_b, pw_w, pw_b)
    mixed = jnp.concatenate([ya * jax.nn.silu(ga), yb * jax.nn.silu(gb), yc * jax.nn.silu(gc)], axis=-1)
    return x + mixed @ w_o, ka_h, va_h, ret_s, conv_s


def setup_inputs(seed: int = 0) -> dict:
    key = jax.random.key(seed)
    ks = jax.random.split(key, 17)
    nrm = jax.random.normal
    f32 = jnp.float32
    l_buf = min(WIN_MAX, PAST_LEN)
    return {
        'x_prompt': nrm(ks[0], (BATCH, SEQ, D_MODEL), f32),
        'x_sample': nrm(ks[1], (DEC_BATCH, DEC_SEQ, D_MODEL), f32),
        'cache_attn_k': nrm(ks[2], (DEPTH, DEC_BATCH, l_buf, A_HEADS, A_HEAD_DIM), f32),
        'cache_attn_v': nrm(ks[3], (DEPTH, DEC_BATCH, l_buf, A_HEADS, A_HEAD_DIM), f32),
        'state_retention': 0.5 * nrm(ks[4], (DEPTH, DEC_BATCH, B_HEADS, B_KEY_DIM, B_VAL_DIM), f32),
        'state_conv': 0.5 * nrm(ks[5], (DEPTH, DEC_BATCH, CONV_W - 1, C_WIDTH), f32),
        'norm_g': 1.0 + 0.01 * nrm(ks[6], (DEPTH, D_MODEL), f32),
        'w_in': nrm(ks[7], (DEPTH, D_MODEL, IN_WIDTH), f32) * D_MODEL ** -0.5,
        'w_out': nrm(ks[8], (DEPTH, MIX_WIDTH, D_MODEL), f32) * MIX_WIDTH ** -0.5,
        'ret_norm_g': 1.0 + 0.01 * nrm(ks[9], (DEPTH, B_WIDTH), f32),
        'conv_dw_w': nrm(ks[10], (DEPTH, CONV_W, C_WIDTH), f32) * CONV_W ** -0.5,
        'conv_dw_b': 0.01 * nrm(ks[11], (DEPTH, C_WIDTH), f32),
        'conv_ln_g': 1.0 + 0.01 * nrm(ks[12], (DEPTH, C_WIDTH), f32),
        'conv_ln_b': 0.01 * nrm(ks[13], (DEPTH, C_WIDTH), f32),
        'conv_pw_w': nrm(ks[14], (DEPTH, C_WIDTH, C_WIDTH), f32) * C_WIDTH ** -0.5,
        'conv_pw_b': 0.01 * nrm(ks[15], (DEPTH, C_WIDTH), f32),
        'final_norm_g': 1.0 + 0.01 * nrm(ks[16], (D_MODEL,), f32),
    }


def reference(x_prompt, x_sample, cache_attn_k, cache_attn_v, state_retention, state_conv,
              norm_g, w_in, w_out, ret_norm_g, conv_dw_w, conv_dw_b, conv_ln_g, conv_ln_b,
              conv_pw_w, conv_pw_b, final_norm_g):
    xp, xs = x_prompt, x_sample
    Bp, S = xp.shape[0], xp.shape[1]
    Bd, T = xs.shape[0], xs.shape[1]
    lbuf_p = min(WIN_MAX, S)
    kp_l, vp_l, ks_l, vs_l, rp_l, rs_l, cp_l, cs_l = [], [], [], [], [], [], [], []
    for l in range(DEPTH):
        p = (norm_g[l], w_in[l], w_out[l], ret_norm_g[l], conv_dw_w[l], conv_dw_b[l],
             conv_ln_g[l], conv_ln_b[l], conv_pw_w[l], conv_pw_b[l])
        xp, kp, vp, rp, cp = _layer(
            xp, p, _dilated_attn_prompt,
            jnp.zeros((Bp, B_HEADS, B_KEY_DIM, B_VAL_DIM), jnp.float32), min(RET_CHUNK, S),
            jnp.zeros((Bp, CONV_W - 1, C_WIDTH), xp.dtype))
        kp_l.append(kp[:, S - lbuf_p:])
        vp_l.append(vp[:, S - lbuf_p:])
        rp_l.append(rp)
        cp_l.append(cp)
        attn_s = functools.partial(_dilated_attn_sample, k_buf=cache_attn_k[l], v_buf=cache_attn_v[l])
        xs, k_s, v_s, r_s, c_s = _layer(xs, p, attn_s, state_retention[l], T, state_conv[l])
        ks_l.append(k_s)
        vs_l.append(v_s)
        rs_l.append(r_s)
        cs_l.append(c_s)
    y_prompt = _rmsnorm(xp, final_norm_g)
    y_sample = _rmsnorm(xs, final_norm_g)
    return (y_prompt, y_sample,
            jnp.stack(kp_l), jnp.stack(vp_l), jnp.stack(ks_l), jnp.stack(vs_l),
            jnp.stack(rp_l), jnp.stack(rs_l), jnp.stack(cp_l), jnp.stack(cs_l))
```

```python
import functools

import jax
import jax.numpy as jnp
from jax import lax
from jax.experimental import pallas as pl
from jax.experimental.pallas import tpu as pltpu

F32 = jnp.float32
BF16 = jnp.bfloat16

D_MODEL = 1024
A_HEADS = 8
A_HEAD_DIM = 64
A_WIDTH = A_HEADS * A_HEAD_DIM
DIL_GROUPS = ((128, 1), (512, 4), (2048, 16))
WIN_MAX = 2048
ATT_BLK = 128
B_HEADS = 4
B_DIM = 64
B_WIDTH = B_HEADS * B_DIM
RET_CHUNK = 128
C_WIDTH = 256
CONV_W = 31
IN_WIDTH = 4 * A_WIDTH + 4 * B_WIDTH + 3 * C_WIDTH
QKV_WIDTH = 3 * A_WIDTH
REST_WIDTH = IN_WIDTH - QKV_WIDTH
EPS = 1e-6
NEG = -1e30
LANES = 128
HEAD_PAIRS = A_HEADS // 2
RET_PAIRS = B_HEADS // 2
VMEM_LIMIT = 48 * 1024 * 1024

REST_GA = 0
REST_QB = 4
REST_KB = 6
REST_VB = 8
REST_GB = 10
REST_UA = 6
REST_UB = 7
REST_GC = 8

CONTRACT_LAST = (((1,), (1,)), ((), ()))
CONTRACT_FIRST = (((0,), (0,)), ((), ()))


def _cparams(semantics):
    return pltpu.CompilerParams(dimension_semantics=semantics, vmem_limit_bytes=VMEM_LIMIT)


def _silu(x):
    return x * jax.nn.sigmoid(x)


def _inproj_body(x_ref, g_ref, w_ref):
    x = x_ref[...]
    h = x * lax.rsqrt(jnp.mean(x * x, axis=-1, keepdims=True) + EPS) * g_ref[...]
    h = h.astype(BF16)

    def cols(c0, c1):
        return jnp.dot(h, w_ref[:, c0:c1], preferred_element_type=F32)

    return cols


def _inproj_prompt_kernel(x_ref, g_ref, w_ref, qh_ref, kh_ref, vh_ref, kf_ref, vf_ref, rest_ref):
    cols = _inproj_body(x_ref, g_ref, w_ref)
    q = cols(0, A_WIDTH) * (A_HEAD_DIM ** -0.5)
    k = cols(A_WIDTH, 2 * A_WIDTH)
    v = cols(2 * A_WIDTH, 3 * A_WIDTH)
    kf_ref[...] = k
    vf_ref[...] = v
    for hp in range(HEAD_PAIRS):
        sl = slice(hp * LANES, (hp + 1) * LANES)
        qh_ref[hp] = q[:, sl].astype(BF16)
        kh_ref[hp] = k[:, sl].astype(BF16)
        vh_ref[hp] = v[:, sl].astype(BF16)
    for c0 in range(0, REST_WIDTH, 768):
        rest_ref[:, c0:c0 + 768] = cols(QKV_WIDTH + c0, QKV_WIDTH + c0 + 768)


def _inproj_prompt(x, g, w, tm=512):
    B, S, D = x.shape
    nb = S // tm
    hm = jax.ShapeDtypeStruct((B, HEAD_PAIRS, S, LANES), BF16)
    hm_spec = pl.BlockSpec((None, HEAD_PAIRS, tm, LANES), lambda b, i: (b, 0, i, 0))
    row = lambda width: pl.BlockSpec((None, tm, width), lambda b, i: (b, i, 0))
    return pl.pallas_call(
        _inproj_prompt_kernel,
        grid=(B, nb),
        in_specs=[row(D),
                  pl.BlockSpec((1, D), lambda b, i: (0, 0)),
                  pl.BlockSpec((D, IN_WIDTH), lambda b, i: (0, 0))],
        out_specs=[hm_spec, hm_spec, hm_spec, row(A_WIDTH), row(A_WIDTH), row(REST_WIDTH)],
        out_shape=[hm, hm, hm,
                   jax.ShapeDtypeStruct((B, S, A_WIDTH), F32),
                   jax.ShapeDtypeStruct((B, S, A_WIDTH), F32),
                   jax.ShapeDtypeStruct((B, S, REST_WIDTH), F32)],
        compiler_params=_cparams(("arbitrary", "arbitrary")),
        name="inproj_prompt",
    )(x, g.reshape(1, D), w)


def _inproj_sample_kernel(x_ref, g_ref, w_ref, q_ref, kf_ref, vf_ref, rest_ref):
    cols = _inproj_body(x_ref, g_ref, w_ref)
    q_ref[...] = cols(0, A_WIDTH) * (A_HEAD_DIM ** -0.5)
    kf_ref[...] = cols(A_WIDTH, 2 * A_WIDTH)
    vf_ref[...] = cols(2 * A_WIDTH, 3 * A_WIDTH)
    for c0 in range(0, REST_WIDTH, 768):
        rest_ref[:, c0:c0 + 768] = cols(QKV_WIDTH + c0, QKV_WIDTH + c0 + 768)


def _inproj_sample(x, g, w, tm=256):
    N, D = x.shape
    row = lambda width: pl.BlockSpec((tm, width), lambda i: (i, 0))
    return pl.pallas_call(
        _inproj_sample_kernel,
        grid=(N // tm,),
        in_specs=[row(D),
                  pl.BlockSpec((1, D), lambda i: (0, 0)),
                  pl.BlockSpec((D, IN_WIDTH), lambda i: (0, 0))],
        out_specs=[row(A_WIDTH), row(A_WIDTH), row(A_WIDTH), row(REST_WIDTH)],
        out_shape=[jax.ShapeDtypeStruct((N, A_WIDTH), F32)] * 3
                  + [jax.ShapeDtypeStruct((N, REST_WIDTH), F32)],
        compiler_params=_cparams(("arbitrary",)),
        name="inproj_sample",
    )(x, g.reshape(1, D), w)


def _outproj_kernel(x_ref, ma_ref, mb_ref, mc_ref, w_ref, g_ref, o_ref, *, final_norm):
    y = x_ref[...]
    y = y + jnp.dot(ma_ref[...], w_ref[0:A_WIDTH, :], preferred_element_type=F32)
    y = y + jnp.dot(mb_ref[...], w_ref[A_WIDTH:A_WIDTH + B_WIDTH, :], preferred_element_type=F32)
    y = y + jnp.dot(mc_ref[...], w_ref[A_WIDTH + B_WIDTH:, :], preferred_element_type=F32)
    if final_norm:
        y = y * lax.rsqrt(jnp.mean(y * y, axis=-1, keepdims=True) + EPS) * g_ref[...]
    o_ref[...] = y


def _outproj(x, ma, mb, mc, w, g_final, final_norm, tm):
    N, D = x.shape
    row = lambda width: pl.BlockSpec((tm, width), lambda i: (i, 0))
    return pl.pallas_call(
        functools.partial(_outproj_kernel, final_norm=final_norm),
        grid=(N // tm,),
        in_specs=[row(D), row(A_WIDTH), row(B_WIDTH), row(C_WIDTH),
                  pl.BlockSpec((D, D), lambda i: (0, 0)),
                  pl.BlockSpec((1, D), lambda i: (0, 0))],
        out_specs=row(D),
        out_shape=jax.ShapeDtypeStruct((N, D), F32),
        compiler_params=_cparams(("arbitrary",)),
        name="outproj",
    )(x, ma, mb, mc, w, g_final.reshape(1, D))


def _alibi_slopes():
    return jnp.exp2(-8.0 * (jnp.arange(A_HEADS, dtype=F32) + 1.0) / A_HEADS)


def _prompt_bias():
    slopes = _alibi_slopes()
    qi = jnp.arange(ATT_BLK)[:, None]
    ki = jnp.arange(2 * ATT_BLK)[None, :] - ATT_BLK
    dist = qi - ki
    tiles = []
    for w, r in DIL_GROUPS:
        n_back = w // r
        valid = (dist >= 0) & (dist <= n_back)
        pen = (slopes * r)[:, None, None] * dist.astype(F32)[None]
        b = jnp.where(valid[None], -pen, NEG)
        tiles.append(b.reshape(HEAD_PAIRS, 2 * ATT_BLK, 2 * ATT_BLK))
    return jnp.stack(tiles)


def _attn_unit(q, kwin, vwin, bias, lo_half):
    zero = jnp.zeros_like(q)
    q2 = jnp.concatenate([jnp.where(lo_half, q, zero), jnp.where(lo_half, zero, q)], axis=0)
    s = lax.dot_general(q2, kwin, CONTRACT_LAST, preferred_element_type=F32) + bias
    m = jnp.max(s, axis=-1, keepdims=True)
    p = jnp.exp(s - m)
    den = jnp.sum(p, axis=-1, keepdims=True)
    o = jnp.dot(p.astype(BF16), vwin, preferred_element_type=F32) / den
    lse = m + jnp.log(den)
    o_pair = jnp.where(lo_half, o[:ATT_BLK], o[ATT_BLK:])
    lse_pair = jnp.where(lo_half, lse[:ATT_BLK], lse[ATT_BLK:])
    return o_pair, lse_pair


def _attn_prompt_kernel(q1_ref, k1_ref, v1_ref, q4_ref, k4_ref, v4_ref, q16_ref, k16_ref, v16_ref,
                        bias_ref, ga_ref, out_ref,
                        o1_ref, l1_ref, o4_ref, l4_ref, o16_ref, l16_ref):
    S = q1_ref.shape[0]
    lo_half = lax.broadcasted_iota(jnp.int32, (ATT_BLK, LANES), 1) < A_HEAD_DIM
    group_refs = ((q1_ref, k1_ref, v1_ref, o1_ref, l1_ref),
                  (q4_ref, k4_ref, v4_ref, o4_ref, l4_ref),
                  (q16_ref, k16_ref, v16_ref, o16_ref, l16_ref))
    for g, (_, r) in enumerate(DIL_GROUPS):
        q_ref, k_ref, v_ref, o_ref, l_ref = group_refs[g]
        nblk = S // r // ATT_BLK
        for c in range(r):
            lanes = slice(c * LANES, (c + 1) * LANES)

            def store(blk, o_pair, lse_pair, r=r, c=c, o_ref=o_ref, l_ref=l_ref):
                if r > 1:
                    rows = pl.ds(blk * (ATT_BLK * r) + c, ATT_BLK, stride=r)
                else:
                    rows = pl.ds(pl.multiple_of(blk * ATT_BLK, ATT_BLK), ATT_BLK)
                o_ref[rows, :] = o_pair
                l_ref[rows, :] = lse_pair

            o_pair, lse_pair = _attn_unit(q_ref[0:ATT_BLK, lanes], k_ref[0:ATT_BLK, lanes],
                                          v_ref[0:ATT_BLK, lanes], bias_ref[g, :, ATT_BLK:], lo_half)
            store(0, o_pair, lse_pair)

            if nblk > 1:
                def body(blk, carry, q_ref=q_ref, k_ref=k_ref, v_ref=v_ref, lanes=lanes, g=g, store=store):
                    q0 = pl.multiple_of(blk * ATT_BLK, ATT_BLK)
                    k0 = pl.multiple_of((blk - 1) * ATT_BLK, ATT_BLK)
                    o_pair, lse_pair = _attn_unit(q_ref[pl.ds(q0, ATT_BLK), lanes],
                                                  k_ref[pl.ds(k0, 2 * ATT_BLK), lanes],
                                                  v_ref[pl.ds(k0, 2 * ATT_BLK), lanes],
                                                  bias_ref[g], lo_half)
                    store(blk, o_pair, lse_pair)
                    return carry

                lax.fori_loop(1, nblk, body, 0)

    rows_per = 256

    def merge(i, carry):
        rows = pl.ds(pl.multiple_of(i * rows_per, rows_per), rows_per)
        l1, l4, l16 = l1_ref[rows, :], l4_ref[rows, :], l16_ref[rows, :]
        mx = jnp.maximum(jnp.maximum(l1, l4), l16)
        e1, e4, e16 = jnp.exp(l1 - mx), jnp.exp(l4 - mx), jnp.exp(l16 - mx)
        tot = e1 + e4 + e16
        y = (e1 / tot) * o1_ref[rows, :] + (e4 / tot) * o4_ref[rows, :] + (e16 / tot) * o16_ref[rows, :]
        out_ref[rows, :] = (y * _silu(ga_ref[rows, :])).astype(BF16)
        return carry

    lax.fori_loop(0, S // rows_per, merge, 0)


def _attn_prompt(qh, kh, vh, rest, bias):
    B, _, S, _ = qh.shape
    ins, specs = [], []
    for _, r in DIL_GROUPS:
        for t in (qh, kh, vh):
            ins.append(t.reshape(B, HEAD_PAIRS, S // r, r * LANES))
            specs.append(pl.BlockSpec((None, None, S // r, r * LANES), lambda b, hp: (b, hp, 0, 0)))
    ins += [bias, rest]
    specs += [pl.BlockSpec((len(DIL_GROUPS), None, 2 * ATT_BLK, 2 * ATT_BLK), lambda b, hp: (0, hp, 0, 0)),
              pl.BlockSpec((None, S, LANES), lambda b, hp: (b, 0, REST_GA + hp))]
    return pl.pallas_call(
        _attn_prompt_kernel,
        grid=(B, HEAD_PAIRS),
        in_specs=specs,
        out_specs=pl.BlockSpec((None, S, LANES), lambda b, hp: (b, 0, hp)),
        out_shape=jax.ShapeDtypeStruct((B, S, A_WIDTH), BF16),
        scratch_shapes=[pltpu.VMEM((S, LANES), F32)] * 6,
        compiler_params=_cparams(("arbitrary", "arbitrary")),
        name="attn_prompt",
    )(*ins)


def _sample_bias(l_buf, T):
    slopes = _alibi_slopes()
    t = jnp.arange(T)[:, None]
    rho = jnp.arange(l_buf)[None, :]
    tn = jnp.arange(LANES)[None, :]
    cache, new = [], []
    for w, r in DIL_GROUPS:
        n_back = w // r
        sl = (slopes * r)[:, None, None]
        dc = l_buf + t - rho
        jc = dc // r
        vc = (dc % r == 0) & (jc <= n_back)
        cache.append(jnp.where(vc[None], -(sl * jc.astype(F32)[None]), NEG).reshape(A_HEADS * T, l_buf))
        dn = t - tn
        jn = dn // r
        vn = (dn >= 0) & (dn % r == 0) & (jn <= n_back)
        new.append(jnp.where(vn[None], -(sl * jn.astype(F32)[None]), NEG).reshape(A_HEADS * T, LANES))
    return jnp.stack(cache), jnp.stack(new)


def _attn_sample_kernel(q_ref, kn_ref, vn_ref, kc_ref, vc_ref, bc_ref, bn_ref, ga_ref, out_ref):
    T = q_ref.shape[0]
    HT = A_HEADS * T
    q = jnp.concatenate([q_ref[...]] * A_HEADS, axis=0)
    row_head = lax.broadcasted_iota(jnp.int32, (HT, A_WIDTH), 0) // T
    lane_head = lax.broadcasted_iota(jnp.int32, (HT, A_WIDTH), 1) // A_HEAD_DIM
    own = row_head == lane_head
    q2 = jnp.where(own, q, 0.0).astype(BF16)
    pad = jnp.zeros((LANES - T, A_WIDTH), F32)
    kn = jnp.concatenate([kn_ref[...], pad], axis=0).astype(BF16)
    vn = jnp.concatenate([vn_ref[...], pad], axis=0).astype(BF16)
    kc = kc_ref[...].astype(BF16)
    sc = lax.dot_general(q2, kc, CONTRACT_LAST, preferred_element_type=F32)
    sn = lax.dot_general(q2, kn, CONTRACT_LAST, preferred_element_type=F32)
    pcs, pns, lses, dens = [], [], [], []
    for g in range(len(DIL_GROUPS)):
        scg = sc + bc_ref[g]
        sng = sn + bn_ref[g]
        m = jnp.maximum(jnp.max(scg, axis=-1, keepdims=True), jnp.max(sng, axis=-1, keepdims=True))
        pc = jnp.exp(scg - m)
        pn = jnp.exp(sng - m)
        den = jnp.sum(pc, axis=-1, keepdims=True) + jnp.sum(pn, axis=-1, keepdims=True)
        pcs.append(pc)
        pns.append(pn)
        dens.append(den)
        lses.append(m + jnp.log(den))
    mx = jnp.maximum(jnp.maximum(lses[0], lses[1]), lses[2])
    es = [jnp.exp(l - mx) for l in lses]
    tot = es[0] + es[1] + es[2]
    coef = [(e / tot) / d for e, d in zip(es, dens)]
    pc = coef[0] * pcs[0] + coef[1] * pcs[1] + coef[2] * pcs[2]
    pn = coef[0] * pns[0] + coef[1] * pns[1] + coef[2] * pns[2]
    o = jnp.dot(pc.astype(BF16), vc_ref[...].astype(BF16), preferred_element_type=F32)
    o = o + jnp.dot(pn.astype(BF16), vn, preferred_element_type=F32)
    o = jnp.where(own, o, 0.0)
    y = o[0:T]
    for h in range(1, A_HEADS):
        y = y + o[h * T:(h + 1) * T]
    out_ref[...] = (y * _silu(ga_ref[...])).astype(BF16)


def _attn_sample(q, kn, vn, cache_k, cache_v, layer, bias_c, bias_n, rest):
    Bd, T, _ = q.shape
    l_buf = cache_k.shape[2]
    HT = A_HEADS * T
    tok = pl.BlockSpec((None, T, A_WIDTH), lambda b: (b, 0, 0))
    cache = pl.BlockSpec((None, None, l_buf, A_WIDTH), lambda b: (layer, b, 0, 0))
    G = len(DIL_GROUPS)
    return pl.pallas_call(
        _attn_sample_kernel,
        grid=(Bd,),
        in_specs=[tok, tok, tok, cache, cache,
                  pl.BlockSpec((G, HT, l_buf), lambda b: (0, 0, 0)),
                  pl.BlockSpec((G, HT, LANES), lambda b: (0, 0, 0)),
                  pl.BlockSpec((None, T, A_WIDTH), lambda b: (b, 0, REST_GA))],
        out_specs=tok,
        out_shape=jax.ShapeDtypeStruct((Bd, T, A_WIDTH), BF16),
        compiler_params=_cparams(("arbitrary",)),
        name="attn_sample",
    )(q, kn, vn, cache_k, cache_v, bias_c, bias_n, rest)


def _retention_consts(chunk, nseq):
    H = B_HEADS
    log_g = jnp.log(1.0 - jnp.exp2(-5.0 - jnp.arange(H, dtype=F32)))
    i = jnp.arange(chunk, dtype=F32)
    diff = i[:, None] - i[None, :]
    dmat = jnp.where(diff >= 0, jnp.exp(jnp.maximum(diff, 0.0)[None] * log_g[:, None, None]), 0.0)
    q_dec = jnp.exp((i[:, None] + 1.0) * log_g[None, :])
    k_dec = jnp.exp((chunk - 1.0 - i)[:, None] * log_g[None, :])
    c_dec = jnp.exp(chunk * log_g)
    rows = nseq * chunk
    eye = jnp.eye(nseq, dtype=F32)
    dmat = (eye[None, :, None, :, None] * dmat[:, None, :, None, :]).reshape(H, rows, rows)
    lanes = lambda t: jnp.repeat(t, B_DIM, axis=-1)
    pair = lambda t: t.reshape(t.shape[0], RET_PAIRS, LANES).transpose(1, 0, 2)
    dmat2 = dmat.reshape(RET_PAIRS, 2 * rows, rows)
    qd = pair(lanes(jnp.tile(q_dec, (nseq, 1))))
    kd = pair(lanes(jnp.tile(k_dec, (nseq, 1))))
    cd = lanes(c_dec[None, :]).reshape(RET_PAIRS, 1, LANES)
    cd = jnp.broadcast_to(cd, (RET_PAIRS, LANES, LANES))
    return dmat2, qd, kd, cd


def _ret_intra(qf, kf, v, dmat, lo_half):
    rows = qf.shape[0]
    q2 = jnp.concatenate([jnp.where(lo_half, qf, 0.0), jnp.where(lo_half, 0.0, qf)], axis=0).astype(BF16)
    att = lax.dot_general(q2, kf.astype(BF16), CONTRACT_LAST, preferred_element_type=F32) * dmat
    oi = jnp.dot(att.astype(BF16), v, preferred_element_type=F32)
    return jnp.where(lo_half, oi[:rows], oi[rows:])


def _ret_norm_gate(o, gn, gate, lo_half):
    def head_sum(x):
        a = jnp.sum(jnp.where(lo_half, x, 0.0), axis=-1, keepdims=True)
        b = jnp.sum(jnp.where(lo_half, 0.0, x), axis=-1, keepdims=True)
        return jnp.where(lo_half, a, b)

    mu = head_sum(o) / B_DIM
    d = o - mu
    var = head_sum(d * d) / B_DIM
    y = d * lax.rsqrt(var + EPS) * gn
    return (y * _silu(gate)).astype(BF16)


def _same_head_mask():
    r2 = lax.broadcasted_iota(jnp.int32, (LANES, LANES), 0) < B_DIM
    c2 = lax.broadcasted_iota(jnp.int32, (LANES, LANES), 1) < B_DIM
    return r2 == c2


def _retention_prompt_kernel(q_ref, k_ref, v_ref, gate_ref, s0_ref, dmat_ref, qd_ref, kd_ref, cd_ref, gn_ref,
                             out_ref, sfin_ref, *, chunk):
    L = q_ref.shape[0]
    lo_half = lax.broadcasted_iota(jnp.int32, (chunk, LANES), 1) < B_DIM
    same_head = _same_head_mask()

    def step(ci, state):
        rows = pl.ds(pl.multiple_of(ci * chunk, chunk), chunk)
        qf = q_ref[rows, :]
        kf = k_ref[rows, :] * (B_DIM ** -0.5)
        v = v_ref[rows, :].astype(BF16)
        o = _ret_intra(qf, kf, v, dmat_ref[...], lo_half)
        o = o + jnp.dot(qf.astype(BF16), state.astype(BF16), preferred_element_type=F32) * qd_ref[...]
        kv = lax.dot_general((kf * kd_ref[...]).astype(BF16), v, CONTRACT_FIRST, preferred_element_type=F32)
        out_ref[rows, :] = _ret_norm_gate(o, gn_ref[...], gate_ref[rows, :], lo_half)
        return state * cd_ref[...] + jnp.where(same_head, kv, 0.0)

    sfin_ref[...] = lax.fori_loop(0, L // chunk, step, s0_ref[...])


def _retention_prompt(rest, s0_bd, consts, gn_g, chunk):
    B, L, _ = rest.shape
    dmat2, qd, kd, cd = consts
    col = lambda blk: pl.BlockSpec((None, L, LANES), lambda b, p, blk=blk: (b, 0, blk + p))
    pair_const = lambda a: pl.BlockSpec((None,) + a.shape[1:], lambda b, p: (p,) + (0,) * (a.ndim - 1))
    st = pl.BlockSpec((None, None, LANES, LANES), lambda b, p: (b, p, 0, 0))
    gn = gn_g.reshape(RET_PAIRS, 1, LANES)
    return pl.pallas_call(
        functools.partial(_retention_prompt_kernel, chunk=chunk),
        grid=(B, RET_PAIRS),
        in_specs=[col(REST_QB), col(REST_KB), col(REST_VB), col(REST_GB), st,
                  pair_const(dmat2), pair_const(qd), pair_const(kd), pair_const(cd), pair_const(gn)],
        out_specs=[pl.BlockSpec((None, L, LANES), lambda b, p: (b, 0, p)), st],
        out_shape=[jax.ShapeDtypeStruct((B, L, B_WIDTH), BF16),
                   jax.ShapeDtypeStruct((B, RET_PAIRS, LANES, LANES), F32)],
        compiler_params=_cparams(("arbitrary", "arbitrary")),
        name="retention_prompt",
    )(rest, rest, rest, rest, s0_bd, dmat2, qd, kd, cd, gn)


def _retention_sample_kernel(q_ref, k_ref, v_ref, gate_ref, s0_ref, dmat_ref, qd_ref, kd_ref, cd_ref, gn_ref,
                             out_ref, sfin_ref, *, T):
    rows = q_ref.shape[0]
    nseq = rows // T
    lo_half = lax.broadcasted_iota(jnp.int32, (rows, LANES), 1) < B_DIM
    seq_of_row = lax.broadcasted_iota(jnp.int32, (rows, LANES), 0) // T

    def expand(x):
        return jnp.concatenate([jnp.where(seq_of_row == s, x, 0.0) for s in range(nseq)], axis=1).astype(BF16)

    qf = q_ref[...]
    kf = k_ref[...] * (B_DIM ** -0.5)
    v = v_ref[...].astype(BF16)
    s0 = s0_ref[...]
    o = _ret_intra(qf, kf, v, dmat_ref[...], lo_half)
    s0_rows = s0.reshape(nseq * LANES, LANES).astype(BF16)
    o = o + jnp.dot(expand(qf), s0_rows, preferred_element_type=F32) * qd_ref[...]
    kv = lax.dot_general(expand(kf * kd_ref[...]), v, CONTRACT_FIRST, preferred_element_type=F32)
    out_ref[...] = _ret_norm_gate(o, gn_ref[...], gate_ref[...], lo_half)
    sfin_ref[...] = s0 * cd_ref[...][None] + jnp.where(_same_head_mask()[None],
                                                      kv.reshape(nseq, LANES, LANES), 0.0)


def _retention_sample(rest, s0_bd, consts, gn_g, T, nseq=16):
    N, _ = rest.shape
    rows = nseq * T
    dmat2, qd, kd, cd = consts
    col = lambda blk: pl.BlockSpec((rows, LANES), lambda i, p, blk=blk: (i, blk + p))
    pair_const = lambda a: pl.BlockSpec((None,) + a.shape[1:], lambda i, p: (p,) + (0,) * (a.ndim - 1))
    st = pl.BlockSpec((nseq, None, LANES, LANES), lambda i, p: (i, p, 0, 0))
    gn = gn_g.reshape(RET_PAIRS, 1, LANES)
    return pl.pallas_call(
        functools.partial(_retention_sample_kernel, T=T),
        grid=(N // rows, RET_PAIRS),
        in_specs=[col(REST_QB), col(REST_KB), col(REST_VB), col(REST_GB), st,
                  pair_const(dmat2), pair_const(qd), pair_const(kd), pair_const(cd), pair_const(gn)],
        out_specs=[pl.BlockSpec((rows, LANES), lambda i, p: (i, p)), st],
        out_shape=[jax.ShapeDtypeStruct((N, B_WIDTH), BF16),
                   jax.ShapeDtypeStruct(s0_bd.shape, F32)],
        compiler_params=_cparams(("arbitrary", "arbitrary")),
        name="retention_sample",
    )(rest, rest, rest, rest, s0_bd, dmat2, qd, kd, cd, gn)


def _state_to_pairs(s):
    B = s.shape[0]
    s = s.reshape(B, RET_PAIRS, 2, B_DIM, B_DIM)
    z = jnp.zeros_like(s[:, :, 0])
    top = jnp.concatenate([s[:, :, 0], z], axis=-1)
    bot = jnp.concatenate([z, s[:, :, 1]], axis=-1)
    return jnp.concatenate([top, bot], axis=-2)


def _pairs_to_state(sbd):
    B = sbd.shape[0]
    a = sbd[:, :, :B_DIM, :B_DIM]
    b = sbd[:, :, B_DIM:, B_DIM:]
    return jnp.stack([a, b], axis=2).reshape(B, B_HEADS, B_DIM, B_DIM)


CONV_STATE = CONV_W - 1
CONV_PAD = 32
LANE_HALVES = C_WIDTH // LANES


def _conv_kernel(ua_ref, ub_ref, gc_ref, s0_ref, dww_ref, dwb_ref, lng_ref, lnb_ref, pww_ref, pwb_ref,
                 out_ref, sfin_ref, ext_ref, *, tile):
    L = ua_ref.shape[0]
    u = ua_ref[...] * jax.nn.sigmoid(ub_ref[...])
    s0 = s0_ref[...]
    for h in range(LANE_HALVES):
        lanes = slice(h * LANES, (h + 1) * LANES)
        ext_ref[h, 0:CONV_PAD, :] = s0[:, lanes]
        ext_ref[h, CONV_PAD:CONV_PAD + L, :] = u[:, lanes]
    for h in range(LANE_HALVES):
        sfin_ref[:, h * LANES:(h + 1) * LANES] = ext_ref[h, L:L + CONV_PAD, :]

    def step(i, carry):
        t0 = pl.multiple_of(i * tile, tile)
        halves = []
        for h in range(LANE_HALVES):
            acc = jnp.zeros((tile, LANES), F32)
            for w in range(CONV_W):
                rows = pl.ds(t0 + (CONV_PAD - CONV_STATE + w), tile, stride=1)
                acc = acc + ext_ref[h, rows, :] * dww_ref[w:w + 1, h * LANES:(h + 1) * LANES]
            halves.append(acc)
        y = jnp.concatenate(halves, axis=1) + dwb_ref[...]
        mu = jnp.mean(y, axis=-1, keepdims=True)
        d = y - mu
        var = jnp.mean(d * d, axis=-1, keepdims=True)
        yn = d * lax.rsqrt(var + EPS) * lng_ref[...] + lnb_ref[...]
        yn = _silu(yn).astype(BF16)
        z = jnp.dot(yn, pww_ref[...], preferred_element_type=F32) + pwb_ref[...]
        rows = pl.ds(t0, tile)
        out_ref[rows, :] = (z * _silu(gc_ref[rows, :])).astype(BF16)
        return carry

    lax.fori_loop(0, L // tile, step, 0)


def _conv(rest, s0, dw_w, dw_b, ln_g, ln_b, pw_w, pw_b, tile):
    B, L, _ = rest.shape
    col = lambda blk: pl.BlockSpec((None, L, C_WIDTH), lambda b, blk=blk: (b, 0, blk))
    st = pl.BlockSpec((None, CONV_PAD, C_WIDTH), lambda b: (b, 0, 0))
    vec = pl.BlockSpec((1, C_WIDTH), lambda b: (0, 0))
    s0 = jnp.pad(s0, ((0, 0), (CONV_PAD - CONV_STATE, 0), (0, 0)))
    out, sfin = pl.pallas_call(
        functools.partial(_conv_kernel, tile=tile),
        grid=(B,),
        in_specs=[col(REST_UA), col(REST_UB), col(REST_GC), st,
                  pl.BlockSpec((CONV_W, C_WIDTH), lambda b: (0, 0)), vec, vec, vec,
                  pl.BlockSpec((C_WIDTH, C_WIDTH), lambda b: (0, 0)), vec],
        out_specs=[pl.BlockSpec((None, L, C_WIDTH), lambda b: (b, 0, 0)), st],
        out_shape=[jax.ShapeDtypeStruct((B, L, C_WIDTH), BF16),
                   jax.ShapeDtypeStruct((B, CONV_PAD, C_WIDTH), F32)],
        scratch_shapes=[pltpu.VMEM((LANE_HALVES, CONV_PAD + L, LANES), F32)],
        compiler_params=_cparams(("arbitrary",)),
        name="conformer_conv",
    )(rest, rest, rest, s0, dw_w, dw_b.reshape(1, -1), ln_g.reshape(1, -1), ln_b.reshape(1, -1),
      pw_w, pw_b.reshape(1, -1))
    return out, sfin[:, CONV_PAD - CONV_STATE:]


def kernel(x_prompt, x_sample, cache_attn_k, cache_attn_v, state_retention, state_conv, norm_g, w_in, w_out,
           ret_norm_g, conv_dw_w, conv_dw_b, conv_ln_g, conv_ln_b, conv_pw_w, conv_pw_b, final_norm_g):
    Bp, S, D = x_prompt.shape
    Bd, T, _ = x_sample.shape
    depth = w_in.shape[0]
    l_buf = cache_attn_k.shape[2]
    assert S % (ATT_BLK * DIL_GROUPS[-1][1]) == 0 and S % RET_CHUNK == 0 and l_buf == WIN_MAX
    lbuf_p = min(WIN_MAX, S)
    ret_nseq = 128 // T
    ck = cache_attn_k.reshape(depth, Bd, l_buf, A_WIDTH)
    cv = cache_attn_v.reshape(depth, Bd, l_buf, A_WIDTH)

    w_in_b = w_in.astype(BF16)
    w_out_b = w_out.astype(BF16)
    pw_b16 = conv_pw_w.astype(BF16)
    bias_p = _prompt_bias()
    bias_c, bias_n = _sample_bias(l_buf, T)
    ret_consts_p = _retention_consts(RET_CHUNK, 1)
    ret_consts_s = _retention_consts(T, ret_nseq)
    zero_ret = jnp.zeros((Bp, RET_PAIRS, LANES, LANES), F32)
    zero_conv = jnp.zeros((Bp, CONV_STATE, C_WIDTH), F32)

    xp = x_prompt
    xs = x_sample.reshape(Bd * T, D)
    kp_l, vp_l, ks_l, vs_l, rp_l, rs_l, cp_l, cs_l = [], [], [], [], [], [], [], []
    for l in range(depth):
        last = l == depth - 1
        conv_p = (conv_dw_w[l], conv_dw_b[l], conv_ln_g[l], conv_ln_b[l], pw_b16[l], conv_pw_b[l])
        qh, kh, vh, kf, vf, rest = _inproj_prompt(xp, norm_g[l], w_in_b[l])
        ma = _attn_prompt(qh, kh, vh, rest, bias_p)
        mb, rp = _retention_prompt(rest, zero_ret, ret_consts_p, ret_norm_g[l], RET_CHUNK)
        mc, cp = _conv(rest, zero_conv, *conv_p, tile=256)
        xp = _outproj(xp.reshape(Bp * S, D), ma.reshape(Bp * S, -1), mb.reshape(Bp * S, -1),
                      mc.reshape(Bp * S, -1), w_out_b[l], final_norm_g, last, tm=512).reshape(Bp, S, D)
        kp_l.append(kf[:, S - lbuf_p:].reshape(Bp, lbuf_p, A_HEADS, A_HEAD_DIM))
        vp_l.append(vf[:, S - lbuf_p:].reshape(Bp, lbuf_p, A_HEADS, A_HEAD_DIM))
        rp_l.append(_pairs_to_state(rp))
        cp_l.append(cp)
        q, kf, vf, rest = _inproj_sample(xs, norm_g[l], w_in_b[l])
        rest3 = rest.reshape(Bd, T, REST_WIDTH)
        tok = lambda a: a.reshape(Bd, T, A_WIDTH)
        ma = _attn_sample(tok(q), tok(kf), tok(vf), ck, cv, l, bias_c, bias_n, rest3)
        mb, rs = _retention_sample(rest, _state_to_pairs(state_retention[l]), ret_consts_s, ret_norm_g[l],
                                   T, ret_nseq)
        mc, cs = _conv(rest3, state_conv[l], *conv_p, tile=T)
        xs = _outproj(xs, ma.reshape(Bd * T, -1), mb, mc.reshape(Bd * T, -1),
                      w_out_b[l], final_norm_g, last, tm=256)
        ks_l.append(kf.reshape(Bd, T, A_HEADS, A_HEAD_DIM))
        vs_l.append(vf.reshape(Bd, T, A_HEADS, A_HEAD_DIM))
        rs_l.append(_pairs_to_state(rs))
        cs_l.append(cs)
    return (xp, xs.reshape(Bd, T, D),
            jnp.stack(kp_l), jnp.stack(vp_l), jnp.stack(ks_l), jnp.stack(vs_l),
            jnp.stack(rp_l), jnp.stack(rs_l), jnp.stack(cp_l), jnp.stack(cs_l))
```

```python
import functools

import jax
import jax.numpy as jnp
from jax import lax
from jax.experimental import pallas as pl
from jax.experimental.pallas import tpu as pltpu

F32 = jnp.float32
BF16 = jnp.bfloat16

D_MODEL = 1024
A_HEADS = 8
A_HEAD_DIM = 64
A_WIDTH = A_HEADS * A_HEAD_DIM
DIL_GROUPS = ((128, 1), (512, 4), (2048, 16))
WIN_MAX = 2048
ATT_BLK = 128
B_HEADS = 4
B_DIM = 64
B_WIDTH = B_HEADS * B_DIM
RET_CHUNK = 128
C_WIDTH = 256
CONV_W = 31
IN_WIDTH = 4 * A_WIDTH + 4 * B_WIDTH + 3 * C_WIDTH
QKV_WIDTH = 3 * A_WIDTH
REST_WIDTH = IN_WIDTH - QKV_WIDTH
EPS = 1e-6
NEG = -1e30
LANES = 128
HEAD_PAIRS = A_HEADS // 2
RET_PAIRS = B_HEADS // 2
VMEM_LIMIT = 48 * 1024 * 1024

REST_GA = 0
REST_QB = 4
REST_KB = 6
REST_VB = 8
REST_GB = 10
REST_UA = 6
REST_UB = 7
REST_GC = 8

CONTRACT_LAST = (((1,), (1,)), ((), ()))
CONTRACT_FIRST = (((0,), (0,)), ((), ()))


def _cparams(semantics):
    return pltpu.CompilerParams(dimension_semantics=semantics, vmem_limit_bytes=VMEM_LIMIT)


def _silu(x):
    return x * jax.nn.sigmoid(x)


def _inproj_body(x_ref, g_ref, w_ref):
    x = x_ref[...]
    h = x * lax.rsqrt(jnp.mean(x * x, axis=-1, keepdims=True) + EPS) * g_ref[...]
    h = h.astype(BF16)

    def cols(c0, c1):
        return jnp.dot(h, w_ref[:, c0:c1], preferred_element_type=F32)

    return cols


def _inproj_prompt_kernel(x_ref, g_ref, w_ref, *refs):
    qkv_refs = (refs[0:3], refs[3:6], refs[6:9])
    kf_ref, vf_ref, rest_ref, stage_ref = refs[9:13]
    tm = x_ref.shape[0]
    cols = _inproj_body(x_ref, g_ref, w_ref)
    q = cols(0, A_WIDTH) * (A_HEAD_DIM ** -0.5)
    k = cols(A_WIDTH, 2 * A_WIDTH)
    v = cols(2 * A_WIDTH, 3 * A_WIDTH)
    kf_ref[...] = k
    vf_ref[...] = v
    for t, val in enumerate((q, k, v)):
        for hp in range(HEAD_PAIRS):
            piece = val[:, hp * LANES:(hp + 1) * LANES]
            slab = t * HEAD_PAIRS + hp
            stage_ref[slab] = piece
            for g, (_, r) in enumerate(DIL_GROUPS):
                out_ref = qkv_refs[t][g]
                if r == 1:
                    out_ref[hp, 0] = piece.astype(BF16)
                else:
                    for c in range(r):
                        out_ref[hp, c] = stage_ref[slab, pl.ds(c, tm // r, stride=r), :].astype(BF16)
    for c0 in range(0, REST_WIDTH, 768):
        rest_ref[:, c0:c0 + 768] = cols(QKV_WIDTH + c0, QKV_WIDTH + c0 + 768)


def _inproj_prompt(x, g, w, tm=512):
    B, S, D = x.shape
    nb = S // tm
    row = lambda width: pl.BlockSpec((None, tm, width), lambda b, i: (b, i, 0))
    grp_shapes, grp_specs = [], []
    for _ in range(3):
        for _, r in DIL_GROUPS:
            grp_shapes.append(jax.ShapeDtypeStruct((B, HEAD_PAIRS, r, S // r, LANES), BF16))
            grp_specs.append(pl.BlockSpec((None, HEAD_PAIRS, r, tm // r, LANES), lambda b, i: (b, 0, 0, i, 0)))
    outs = pl.pallas_call(
        _inproj_prompt_kernel,
        grid=(B, nb),
        in_specs=[row(D),
                  pl.BlockSpec((1, D), lambda b, i: (0, 0)),
                  pl.BlockSpec((D, IN_WIDTH), lambda b, i: (0, 0))],
        out_specs=grp_specs + [row(A_WIDTH), row(A_WIDTH), row(REST_WIDTH)],
        out_shape=grp_shapes + [jax.ShapeDtypeStruct((B, S, A_WIDTH), F32),
                                jax.ShapeDtypeStruct((B, S, A_WIDTH), F32),
                                jax.ShapeDtypeStruct((B, S, REST_WIDTH), F32)],
        scratch_shapes=[pltpu.VMEM((3 * HEAD_PAIRS, tm, LANES), F32)],
        compiler_params=_cparams(("arbitrary", "arbitrary")),
        name="inproj_prompt",
    )(x, g.reshape(1, D), w)
    return outs[0:9], outs[9], outs[10], outs[11]


def _inproj_sample_kernel(x_ref, g_ref, w_ref, q_ref, kf_ref, vf_ref, rest_ref):
    cols = _inproj_body(x_ref, g_ref, w_ref)
    q_ref[...] = cols(0, A_WIDTH) * (A_HEAD_DIM ** -0.5)
    kf_ref[...] = cols(A_WIDTH, 2 * A_WIDTH)
    vf_ref[...] = cols(2 * A_WIDTH, 3 * A_WIDTH)
    for c0 in range(0, REST_WIDTH, 768):
        rest_ref[:, c0:c0 + 768] = cols(QKV_WIDTH + c0, QKV_WIDTH + c0 + 768)


def _inproj_sample(x, g, w, tm=256):
    N, D = x.shape
    row = lambda width: pl.BlockSpec((tm, width), lambda i: (i, 0))
    return pl.pallas_call(
        _inproj_sample_kernel,
        grid=(N // tm,),
        in_specs=[row(D),
                  pl.BlockSpec((1, D), lambda i: (0, 0)),
                  pl.BlockSpec((D, IN_WIDTH), lambda i: (0, 0))],
        out_specs=[row(A_WIDTH), row(A_WIDTH), row(A_WIDTH), row(REST_WIDTH)],
        out_shape=[jax.ShapeDtypeStruct((N, A_WIDTH), F32)] * 3
                  + [jax.ShapeDtypeStruct((N, REST_WIDTH), F32)],
        compiler_params=_cparams(("arbitrary",)),
        name="inproj_sample",
    )(x, g.reshape(1, D), w)


def _outproj_kernel(x_ref, ma_ref, mb_ref, mc_ref, w_ref, g_ref, o_ref, *, final_norm):
    y = x_ref[...]
    y = y + jnp.dot(ma_ref[...], w_ref[0:A_WIDTH, :], preferred_element_type=F32)
    y = y + jnp.dot(mb_ref[...], w_ref[A_WIDTH:A_WIDTH + B_WIDTH, :], preferred_element_type=F32)
    y = y + jnp.dot(mc_ref[...], w_ref[A_WIDTH + B_WIDTH:, :], preferred_element_type=F32)
    if final_norm:
        y = y * lax.rsqrt(jnp.mean(y * y, axis=-1, keepdims=True) + EPS) * g_ref[...]
    o_ref[...] = y


def _outproj(x, ma, mb, mc, w, g_final, final_norm, tm):
    N, D = x.shape
    row = lambda width: pl.BlockSpec((tm, width), lambda i: (i, 0))
    return pl.pallas_call(
        functools.partial(_outproj_kernel, final_norm=final_norm),
        grid=(N // tm,),
        in_specs=[row(D), row(A_WIDTH), row(B_WIDTH), row(C_WIDTH),
                  pl.BlockSpec((D, D), lambda i: (0, 0)),
                  pl.BlockSpec((1, D), lambda i: (0, 0))],
        out_specs=row(D),
        out_shape=jax.ShapeDtypeStruct((N, D), F32),
        compiler_params=_cparams(("arbitrary",)),
        name="outproj",
    )(x, ma, mb, mc, w, g_final.reshape(1, D))


def _alibi_slopes():
    return jnp.exp2(-8.0 * (jnp.arange(A_HEADS, dtype=F32) + 1.0) / A_HEADS)


def _prompt_bias():
    slopes = _alibi_slopes()
    qi = jnp.arange(ATT_BLK)[:, None]
    ki = jnp.arange(2 * ATT_BLK)[None, :] - ATT_BLK
    dist = qi - ki
    tiles = []
    for w, r in DIL_GROUPS:
        n_back = w // r
        valid = (dist >= 0) & (dist <= n_back)
        pen = (slopes * r)[:, None, None] * dist.astype(F32)[None]
        b = jnp.where(valid[None], -pen, NEG)
        tiles.append(b.reshape(HEAD_PAIRS, 2 * ATT_BLK, 2 * ATT_BLK))
    normal = jnp.stack(tiles)
    first = jnp.concatenate([normal[..., ATT_BLK:], jnp.full_like(normal[..., ATT_BLK:], NEG)], axis=-1)
    return jnp.stack([normal, first])


def _attn_unit(q, kwin, vwin, bias, lo_half):
    zero = jnp.zeros_like(q)
    q2 = jnp.concatenate([jnp.where(lo_half, q, zero), jnp.where(lo_half, zero, q)], axis=0)
    s = lax.dot_general(q2, kwin, CONTRACT_LAST, preferred_element_type=F32) + bias
    m = jnp.max(s, axis=-1, keepdims=True)
    p = jnp.exp(s - m)
    den = jnp.sum(p, axis=-1, keepdims=True)
    o = jnp.dot(p.astype(BF16), vwin, preferred_element_type=F32) / den
    lse = m + jnp.log(den)
    o_pair = jnp.where(lo_half, o[:ATT_BLK], o[ATT_BLK:])
    lse_pair = jnp.where(lo_half, lse[:ATT_BLK], lse[ATT_BLK:])
    return o_pair, lse_pair


ATTN_UNROLL = 8


def _attn_prompt_kernel(*refs):
    n_groups = len(DIL_GROUPS)
    qkv = refs[0:3 * n_groups]
    bias_ref, ga_ref, out_ref = refs[3 * n_groups:3 * n_groups + 3]
    o_refs = refs[3 * n_groups + 3:4 * n_groups + 3]
    l_refs = refs[4 * n_groups + 3:5 * n_groups + 3]
    S = out_ref.shape[0]
    lo_half = lax.broadcasted_iota(jnp.int32, (ATT_BLK, LANES), 1) < A_HEAD_DIM
    for g, (_, r) in enumerate(DIL_GROUPS):
        q_ref, k_ref, v_ref = qkv[g], qkv[n_groups + g], qkv[2 * n_groups + g]
        o_ref, l_ref = o_refs[g], l_refs[g]
        nblk = S // r // ATT_BLK

        def trip(it, carry, g=g, r=r, nblk=nblk, q_ref=q_ref, k_ref=k_ref, v_ref=v_ref, o_ref=o_ref, l_ref=l_ref):
            for u in range(ATTN_UNROLL):
                n = it * ATTN_UNROLL + u
                blk, c = (n, 0) if r == 1 else (lax.div(n, jnp.int32(r)), lax.rem(n, jnp.int32(r)))
                q0 = pl.multiple_of(blk * ATT_BLK, ATT_BLK)
                q = q_ref[c, pl.ds(q0, ATT_BLK), :]
                if nblk == 1:
                    kwin, vwin = k_ref[c], v_ref[c]
                    bias = bias_ref[0, g, :, ATT_BLK:]
                else:
                    k0 = pl.multiple_of(jnp.maximum(blk - 1, 0) * ATT_BLK, ATT_BLK)
                    kwin = k_ref[c, pl.ds(k0, 2 * ATT_BLK), :]
                    vwin = v_ref[c, pl.ds(k0, 2 * ATT_BLK), :]
                    bias = bias_ref[jnp.where(blk == 0, 1, 0), g]
                o_pair, lse_pair = _attn_unit(q, kwin, vwin, bias, lo_half)
                rows = pl.ds(q0, ATT_BLK) if r == 1 else pl.ds(blk * (ATT_BLK * r) + c, ATT_BLK, stride=r)
                o_ref[rows, :] = o_pair
                l_ref[rows, :] = lse_pair
            return carry

        lax.fori_loop(0, (r * nblk) // ATTN_UNROLL, trip, 0)

    o1_ref, o4_ref, o16_ref = o_refs
    l1_ref, l4_ref, l16_ref = l_refs
    rows_per = 256

    def merge(i, carry):
        rows = pl.ds(pl.multiple_of(i * rows_per, rows_per), rows_per)
        l1, l4, l16 = l1_ref[rows, :], l4_ref[rows, :], l16_ref[rows, :]
        mx = jnp.maximum(jnp.maximum(l1, l4), l16)
        e1, e4, e16 = jnp.exp(l1 - mx), jnp.exp(l4 - mx), jnp.exp(l16 - mx)
        tot = e1 + e4 + e16
        y = (e1 / tot) * o1_ref[rows, :] + (e4 / tot) * o4_ref[rows, :] + (e16 / tot) * o16_ref[rows, :]
        out_ref[rows, :] = (y * _silu(ga_ref[rows, :])).astype(BF16)
        return carry

    lax.fori_loop(0, S // rows_per, merge, 0)


def _attn_prompt(qkv_groups, rest, bias):
    B, S, _ = rest.shape
    assert (S // ATT_BLK) % ATTN_UNROLL == 0
    ins = list(qkv_groups)
    specs = [pl.BlockSpec((None, None) + t.shape[2:], lambda b, hp: (b, hp, 0, 0, 0)) for t in ins]
    ins += [bias, rest]
    specs += [pl.BlockSpec((2, len(DIL_GROUPS), None, 2 * ATT_BLK, 2 * ATT_BLK), lambda b, hp: (0, 0, hp, 0, 0)),
              pl.BlockSpec((None, S, LANES), lambda b, hp: (b, 0, REST_GA + hp))]
    return pl.pallas_call(
        _attn_prompt_kernel,
        grid=(B, HEAD_PAIRS),
        in_specs=specs,
        out_specs=pl.BlockSpec((None, S, LANES), lambda b, hp: (b, 0, hp)),
        out_shape=jax.ShapeDtypeStruct((B, S, A_WIDTH), BF16),
        scratch_shapes=[pltpu.VMEM((S, LANES), F32)] * 6,
        compiler_params=_cparams(("arbitrary", "arbitrary")),
        name="attn_prompt",
    )(*ins)


def _sample_bias(l_buf, T):
    slopes = _alibi_slopes()
    t = jnp.arange(T)[:, None]
    rho = jnp.arange(l_buf)[None, :]
    tn = jnp.arange(LANES)[None, :]
    cache, new = [], []
    for w, r in DIL_GROUPS:
        n_back = w // r
        sl = (slopes * r)[:, None, None]
        dc = l_buf + t - rho
        jc = dc // r
        vc = (dc % r == 0) & (jc <= n_back)
        cache.append(jnp.where(vc[None], -(sl * jc.astype(F32)[None]), NEG).reshape(A_HEADS * T, l_buf))
        dn = t - tn
        jn = dn // r
        vn = (dn >= 0) & (dn % r == 0) & (jn <= n_back)
        new.append(jnp.where(vn[None], -(sl * jn.astype(F32)[None]), NEG).reshape(A_HEADS * T, LANES))
    return jnp.stack(cache), jnp.stack(new)


def _attn_sample_kernel(q_ref, kn_ref, vn_ref, kc_ref, vc_ref, bc_ref, bn_ref, ga_ref, out_ref):
    T = q_ref.shape[0]
    l_buf = kc_ref.shape[-1]
    lo_half = lax.broadcasted_iota(jnp.int32, (T, LANES), 1) < A_HEAD_DIM
    pad = jnp.zeros((LANES - T, A_WIDTH), F32)
    kn = jnp.concatenate([kn_ref[...], pad], axis=0).astype(BF16)
    vn = jnp.concatenate([vn_ref[...], pad], axis=0).astype(BF16)
    q = q_ref[...]
    sc_parts, sn_parts = [], []
    for hp in range(HEAD_PAIRS):
        lanes = slice(hp * LANES, (hp + 1) * LANES)
        qp = q[:, lanes]
        q2 = jnp.concatenate([jnp.where(lo_half, qp, 0.0), jnp.where(lo_half, 0.0, qp)], axis=0).astype(BF16)
        kt = kc_ref[2 * hp:2 * hp + 2].reshape(LANES, l_buf).astype(BF16)
        sc_parts.append(jnp.dot(q2, kt, preferred_element_type=F32))
        sn_parts.append(lax.dot_general(q2, kn[:, lanes], CONTRACT_LAST, preferred_element_type=F32))
    sc = jnp.concatenate(sc_parts, axis=0)
    sn = jnp.concatenate(sn_parts, axis=0)
    pcs, pns, lses, dens = [], [], [], []
    for g in range(len(DIL_GROUPS)):
        scg = sc + bc_ref[g]
        sng = sn + bn_ref[g]
        m = jnp.maximum(jnp.max(scg, axis=-1, keepdims=True), jnp.max(sng, axis=-1, keepdims=True))
        pc = jnp.exp(scg - m)
        pn = jnp.exp(sng - m)
        den = jnp.sum(pc, axis=-1, keepdims=True) + jnp.sum(pn, axis=-1, keepdims=True)
        pcs.append(pc)
        pns.append(pn)
        dens.append(den)
        lses.append(m + jnp.log(den))
    mx = jnp.maximum(jnp.maximum(lses[0], lses[1]), lses[2])
    es = [jnp.exp(l - mx) for l in lses]
    tot = es[0] + es[1] + es[2]
    coef = [(e / tot) / d for e, d in zip(es, dens)]
    pc = coef[0] * pcs[0] + coef[1] * pcs[1] + coef[2] * pcs[2]
    pn = coef[0] * pns[0] + coef[1] * pns[1] + coef[2] * pns[2]
    pc = pc.astype(BF16)
    pn = pn.astype(BF16)
    ys = []
    for hp in range(HEAD_PAIRS):
        lanes = slice(hp * LANES, (hp + 1) * LANES)
        rows = slice(2 * T * hp, 2 * T * (hp + 1))
        vt = vc_ref[2 * hp:2 * hp + 2].reshape(LANES, l_buf).astype(BF16)
        o = lax.dot_general(pc[rows], vt, CONTRACT_LAST, preferred_element_type=F32)
        o = o + jnp.dot(pn[rows], vn[:, lanes], preferred_element_type=F32)
        ys.append(jnp.where(lo_half, o[:T], o[T:]))
    y = jnp.concatenate(ys, axis=1)
    out_ref[...] = (y * _silu(ga_ref[...])).astype(BF16)


def _attn_sample(q, kn, vn, cache_kt, cache_vt, layer, bias_c, bias_n, rest):
    Bd, T, _ = q.shape
    l_buf = cache_kt.shape[-1]
    HT = A_HEADS * T
    tok = pl.BlockSpec((None, T, A_WIDTH), lambda b: (b, 0, 0))
    cache = pl.BlockSpec((None, None, A_HEADS, A_HEAD_DIM, l_buf), lambda b: (layer, b, 0, 0, 0))
    G = len(DIL_GROUPS)
    return pl.pallas_call(
        _attn_sample_kernel,
        grid=(Bd,),
        in_specs=[tok, tok, tok, cache, cache,
                  pl.BlockSpec((G, HT, l_buf), lambda b: (0, 0, 0)),
                  pl.BlockSpec((G, HT, LANES), lambda b: (0, 0, 0)),
                  pl.BlockSpec((None, T, A_WIDTH), lambda b: (b, 0, REST_GA))],
        out_specs=tok,
        out_shape=jax.ShapeDtypeStruct((Bd, T, A_WIDTH), BF16),
        compiler_params=_cparams(("arbitrary",)),
        name="attn_sample",
    )(q, kn, vn, cache_kt, cache_vt, bias_c, bias_n, rest)


def _retention_consts(chunk, nseq):
    H = B_HEADS
    log_g = jnp.log(1.0 - jnp.exp2(-5.0 - jnp.arange(H, dtype=F32)))
    i = jnp.arange(chunk, dtype=F32)
    diff = i[:, None] - i[None, :]
    dmat = jnp.where(diff >= 0, jnp.exp(jnp.maximum(diff, 0.0)[None] * log_g[:, None, None]), 0.0)
    q_dec = jnp.exp((i[:, None] + 1.0) * log_g[None, :])
    k_dec = jnp.exp((chunk - 1.0 - i)[:, None] * log_g[None, :])
    c_dec = jnp.exp(chunk * log_g)
    rows = nseq * chunk
    eye = jnp.eye(nseq, dtype=F32)
    dmat = (eye[None, :, None, :, None] * dmat[:, None, :, None, :]).reshape(H, rows, rows)
    lanes = lambda t: jnp.repeat(t, B_DIM, axis=-1)
    pair = lambda t: t.reshape(t.shape[0], RET_PAIRS, LANES).transpose(1, 0, 2)
    dmat2 = dmat.reshape(RET_PAIRS, 2 * rows, rows)
    qd = pair(lanes(jnp.tile(q_dec, (nseq, 1))))
    kd = pair(lanes(jnp.tile(k_dec, (nseq, 1))))
    cd = lanes(c_dec[None, :]).reshape(RET_PAIRS, 1, LANES)
    cd = jnp.broadcast_to(cd, (RET_PAIRS, LANES, LANES))
    return dmat2, qd, kd, cd


def _ret_intra(qf, kf, v, dmat, lo_half):
    rows = qf.shape[0]
    q2 = jnp.concatenate([jnp.where(lo_half, qf, 0.0), jnp.where(lo_half, 0.0, qf)], axis=0).astype(BF16)
    att = lax.dot_general(q2, kf.astype(BF16), CONTRACT_LAST, preferred_element_type=F32) * dmat
    oi = jnp.dot(att.astype(BF16), v, preferred_element_type=F32)
    return jnp.where(lo_half, oi[:rows], oi[rows:])


def _ret_norm_gate(o, gn, gate, lo_half):
    def head_sum(x):
        a = jnp.sum(jnp.where(lo_half, x, 0.0), axis=-1, keepdims=True)
        b = jnp.sum(jnp.where(lo_half, 0.0, x), axis=-1, keepdims=True)
        return jnp.where(lo_half, a, b)

    mu = head_sum(o) / B_DIM
    d = o - mu
    var = head_sum(d * d) / B_DIM
    y = d * lax.rsqrt(var + EPS) * gn
    return (y * _silu(gate)).astype(BF16)


def _same_head_mask():
    r2 = lax.broadcasted_iota(jnp.int32, (LANES, LANES), 0) < B_DIM
    c2 = lax.broadcasted_iota(jnp.int32, (LANES, LANES), 1) < B_DIM
    return r2 == c2


RET_UNROLL = 4


def _retention_prompt_kernel(q_ref, k_ref, v_ref, gate_ref, s0_ref, dmat_ref, qd_ref, kd_ref, cd_ref, gn_ref,
                             out_ref, sfin_ref, oi_ref, kv_ref, st_ref, *, chunk):
    L = q_ref.shape[0]
    nc = L // chunk
    lo_half = lax.broadcasted_iota(jnp.int32, (chunk, LANES), 1) < B_DIM
    same_head = _same_head_mask()

    def chunk_rows(ci):
        return pl.ds(pl.multiple_of(ci * chunk, chunk), chunk)

    def phase1(it, carry):
        for u in range(RET_UNROLL):
            ci = it * RET_UNROLL + u
            rows = chunk_rows(ci)
            qf = q_ref[rows, :]
            kf = k_ref[rows, :] * (B_DIM ** -0.5)
            v = v_ref[rows, :].astype(BF16)
            oi_ref[rows, :] = _ret_intra(qf, kf, v, dmat_ref[...], lo_half)
            kv = lax.dot_general((kf * kd_ref[...]).astype(BF16), v, CONTRACT_FIRST, preferred_element_type=F32)
            kv_ref[ci] = jnp.where(same_head, kv, 0.0)
        return carry

    lax.fori_loop(0, nc // RET_UNROLL, phase1, 0)

    def phase2(ci, state):
        st_ref[ci] = state.astype(BF16)
        return state * cd_ref[...] + kv_ref[ci]

    sfin_ref[...] = lax.fori_loop(0, nc, phase2, s0_ref[...])

    def phase3(it, carry):
        for u in range(RET_UNROLL):
            ci = it * RET_UNROLL + u
            rows = chunk_rows(ci)
            inter = jnp.dot(q_ref[rows, :].astype(BF16), st_ref[ci], preferred_element_type=F32) * qd_ref[...]
            out_ref[rows, :] = _ret_norm_gate(oi_ref[rows, :] + inter, gn_ref[...], gate_ref[rows, :], lo_half)
        return carry

    lax.fori_loop(0, nc // RET_UNROLL, phase3, 0)


def _retention_prompt(rest, s0_bd, consts, gn_g, chunk):
    B, L, _ = rest.shape
    dmat2, qd, kd, cd = consts
    col = lambda blk: pl.BlockSpec((None, L, LANES), lambda b, p, blk=blk: (b, 0, blk + p))
    pair_const = lambda a: pl.BlockSpec((None,) + a.shape[1:], lambda b, p: (p,) + (0,) * (a.ndim - 1))
    st = pl.BlockSpec((None, None, LANES, LANES), lambda b, p: (b, p, 0, 0))
    gn = gn_g.reshape(RET_PAIRS, 1, LANES)
    return pl.pallas_call(
        functools.partial(_retention_prompt_kernel, chunk=chunk),
        grid=(B, RET_PAIRS),
        in_specs=[col(REST_QB), col(REST_KB), col(REST_VB), col(REST_GB), st,
                  pair_const(dmat2), pair_const(qd), pair_const(kd), pair_const(cd), pair_const(gn)],
        out_specs=[pl.BlockSpec((None, L, LANES), lambda b, p: (b, 0, p)), st],
        out_shape=[jax.ShapeDtypeStruct((B, L, B_WIDTH), BF16),
                   jax.ShapeDtypeStruct((B, RET_PAIRS, LANES, LANES), F32)],
        scratch_shapes=[pltpu.VMEM((L, LANES), F32),
                        pltpu.VMEM((L // chunk, LANES, LANES), F32),
                        pltpu.VMEM((L // chunk, LANES, LANES), BF16)],
        compiler_params=_cparams(("arbitrary", "arbitrary")),
        name="retention_prompt",
    )(rest, rest, rest, rest, s0_bd, dmat2, qd, kd, cd, gn)


def _retention_sample_kernel(q_ref, k_ref, v_ref, gate_ref, s0_ref, dmat_ref, qd_ref, kd_ref, cd_ref, gn_ref,
                             out_ref, sfin_ref, *, T):
    rows = q_ref.shape[0]
    nseq = rows // T
    lo_half = lax.broadcasted_iota(jnp.int32, (rows, LANES), 1) < B_DIM
    seq_of_row = lax.broadcasted_iota(jnp.int32, (rows, LANES), 0) // T

    def expand(x):
        return jnp.concatenate([jnp.where(seq_of_row == s, x, 0.0) for s in range(nseq)], axis=1).astype(BF16)

    qf = q_ref[...]
    kf = k_ref[...] * (B_DIM ** -0.5)
    v = v_ref[...].astype(BF16)
    s0 = s0_ref[...]
    o = _ret_intra(qf, kf, v, dmat_ref[...], lo_half)
    s0_rows = s0.reshape(nseq * LANES, LANES).astype(BF16)
    o = o + jnp.dot(expand(qf), s0_rows, preferred_element_type=F32) * qd_ref[...]
    kv = lax.dot_general(expand(kf * kd_ref[...]), v, CONTRACT_FIRST, preferred_element_type=F32)
    out_ref[...] = _ret_norm_gate(o, gn_ref[...], gate_ref[...], lo_half)
    sfin_ref[...] = s0 * cd_ref[...][None] + jnp.where(_same_head_mask()[None],
                                                      kv.reshape(nseq, LANES, LANES), 0.0)


def _retention_sample(rest, s0_bd, consts, gn_g, T, nseq=16):
    N, _ = rest.shape
    rows = nseq * T
    dmat2, qd, kd, cd = consts
    col = lambda blk: pl.BlockSpec((rows, LANES), lambda i, p, blk=blk: (i, blk + p))
    pair_const = lambda a: pl.BlockSpec((None,) + a.shape[1:], lambda i, p: (p,) + (0,) * (a.ndim - 1))
    st = pl.BlockSpec((nseq, None, LANES, LANES), lambda i, p: (i, p, 0, 0))
    gn = gn_g.reshape(RET_PAIRS, 1, LANES)
    return pl.pallas_call(
        functools.partial(_retention_sample_kernel, T=T),
        grid=(N // rows, RET_PAIRS),
        in_specs=[col(REST_QB), col(REST_KB), col(REST_VB), col(REST_GB), st,
                  pair_const(dmat2), pair_const(qd), pair_const(kd), pair_const(cd), pair_const(gn)],
        out_specs=[pl.BlockSpec((rows, LANES), lambda i, p: (i, p)), st],
        out_shape=[jax.ShapeDtypeStruct((N, B_WIDTH), BF16),
                   jax.ShapeDtypeStruct(s0_bd.shape, F32)],
        compiler_params=_cparams(("arbitrary", "arbitrary")),
        name="retention_sample",
    )(rest, rest, rest, rest, s0_bd, dmat2, qd, kd, cd, gn)


def _state_to_pairs(s):
    B = s.shape[0]
    s = s.reshape(B, RET_PAIRS, 2, B_DIM, B_DIM)
    z = jnp.zeros_like(s[:, :, 0])
    top = jnp.concatenate([s[:, :, 0], z], axis=-1)
    bot = jnp.concatenate([z, s[:, :, 1]], axis=-1)
    return jnp.concatenate([top, bot], axis=-2)


def _pairs_to_state(sbd):
    B = sbd.shape[0]
    a = sbd[:, :, :B_DIM, :B_DIM]
    b = sbd[:, :, B_DIM:, B_DIM:]
    return jnp.stack([a, b], axis=2).reshape(B, B_HEADS, B_DIM, B_DIM)


CONV_STATE = CONV_W - 1
CONV_PAD = 32
LANE_HALVES = C_WIDTH // LANES


def _conv_kernel(ua_ref, ub_ref, gc_ref, s0_ref, dww_ref, dwb_ref, lng_ref, lnb_ref, pww_ref, pwb_ref,
                 out_ref, sfin_ref, ext_ref, *, tile):
    L = ua_ref.shape[0]
    u = ua_ref[...] * jax.nn.sigmoid(ub_ref[...])
    s0 = s0_ref[...]
    for h in range(LANE_HALVES):
        lanes = slice(h * LANES, (h + 1) * LANES)
        ext_ref[h, 0:CONV_PAD, :] = s0[:, lanes]
        ext_ref[h, CONV_PAD:CONV_PAD + L, :] = u[:, lanes]
    for h in range(LANE_HALVES):
        sfin_ref[:, h * LANES:(h + 1) * LANES] = ext_ref[h, L:L + CONV_PAD, :]

    def step(i, carry):
        t0 = pl.multiple_of(i * tile, tile)
        halves = []
        for h in range(LANE_HALVES):
            acc = jnp.zeros((tile, LANES), F32)
            for w in range(CONV_W):
                rows = pl.ds(t0 + (CONV_PAD - CONV_STATE + w), tile, stride=1)
                acc = acc + ext_ref[h, rows, :] * dww_ref[w:w + 1, h * LANES:(h + 1) * LANES]
            halves.append(acc)
        y = jnp.concatenate(halves, axis=1) + dwb_ref[...]
        mu = jnp.mean(y, axis=-1, keepdims=True)
        d = y - mu
        var = jnp.mean(d * d, axis=-1, keepdims=True)
        yn = d * lax.rsqrt(var + EPS) * lng_ref[...] + lnb_ref[...]
        yn = _silu(yn).astype(BF16)
        z = jnp.dot(yn, pww_ref[...], preferred_element_type=F32) + pwb_ref[...]
        rows = pl.ds(t0, tile)
        out_ref[rows, :] = (z * _silu(gc_ref[rows, :])).astype(BF16)
        return carry

    lax.fori_loop(0, L // tile, step, 0)


def _conv(rest, s0, dw_w, dw_b, ln_g, ln_b, pw_w, pw_b, tile):
    B, L, _ = rest.shape
    col = lambda blk: pl.BlockSpec((None, L, C_WIDTH), lambda b, blk=blk: (b, 0, blk))
    st = pl.BlockSpec((None, CONV_PAD, C_WIDTH), lambda b: (b, 0, 0))
    vec = pl.BlockSpec((1, C_WIDTH), lambda b: (0, 0))
    s0 = jnp.pad(s0, ((0, 0), (CONV_PAD - CONV_STATE, 0), (0, 0)))
    out, sfin = pl.pallas_call(
        functools.partial(_conv_kernel, tile=tile),
        grid=(B,),
        in_specs=[col(REST_UA), col(REST_UB), col(REST_GC), st,
                  pl.BlockSpec((CONV_W, C_WIDTH), lambda b: (0, 0)), vec, vec, vec,
                  pl.BlockSpec((C_WIDTH, C_WIDTH), lambda b: (0, 0)), vec],
        out_specs=[pl.BlockSpec((None, L, C_WIDTH), lambda b: (b, 0, 0)), st],
        out_shape=[jax.ShapeDtypeStruct((B, L, C_WIDTH), BF16),
                   jax.ShapeDtypeStruct((B, CONV_PAD, C_WIDTH), F32)],
        scratch_shapes=[pltpu.VMEM((LANE_HALVES, CONV_PAD + L, LANES), F32)],
        compiler_params=_cparams(("arbitrary",)),
        name="conformer_conv",
    )(rest, rest, rest, s0, dw_w, dw_b.reshape(1, -1), ln_g.reshape(1, -1), ln_b.reshape(1, -1),
      pw_w, pw_b.reshape(1, -1))
    return out, sfin[:, CONV_PAD - CONV_STATE:]


def _conv_sample_kernel(ua_ref, ub_ref, gc_ref, s0_ref, dww_ref, dwb_ref, lng_ref, lnb_ref, pww_ref, pwb_ref,
                        out_ref, sfin_ref, u_ref, ut_ref, z_ref, *, T):
    N = ua_ref.shape[0]
    Bd = N // T
    u = ua_ref[...] * jax.nn.sigmoid(ub_ref[...])
    for h in range(LANE_HALVES):
        u_ref[h] = u[:, h * LANES:(h + 1) * LANES]
    for t in range(T):
        ut_ref[t] = jnp.concatenate([u_ref[h, pl.ds(t, Bd, stride=T), :] for h in range(LANE_HALVES)], axis=1)

    def ext(tau):
        return s0_ref[tau] if tau < CONV_STATE else ut_ref[tau - CONV_STATE]

    for tau in range(CONV_STATE):
        sfin_ref[tau] = ext(tau + T)
    for t in range(T):
        acc = jnp.zeros((Bd, C_WIDTH), F32)
        for w in range(CONV_W):
            acc = acc + ext(t + w) * dww_ref[w:w + 1, :]
        y = acc + dwb_ref[...]
        mu = jnp.mean(y, axis=-1, keepdims=True)
        d = y - mu
        var = jnp.mean(d * d, axis=-1, keepdims=True)
        yn = d * lax.rsqrt(var + EPS) * lng_ref[...] + lnb_ref[...]
        yn = _silu(yn).astype(BF16)
        z = jnp.dot(yn, pww_ref[...], preferred_element_type=F32) + pwb_ref[...]
        for h in range(LANE_HALVES):
            z_ref[h, pl.ds(t, Bd, stride=T), :] = z[:, h * LANES:(h + 1) * LANES]
    z = jnp.concatenate([z_ref[h] for h in range(LANE_HALVES)], axis=1)
    out_ref[...] = (z * _silu(gc_ref[...])).astype(BF16)


def _conv_sample(rest, state_t, layer, dw_w, dw_b, ln_g, ln_b, pw_w, pw_b, T):
    N, _ = rest.shape
    Bd = N // T
    col = lambda blk: pl.BlockSpec((N, C_WIDTH), lambda i, blk=blk: (0, blk))
    vec = pl.BlockSpec((1, C_WIDTH), lambda i: (0, 0))
    return pl.pallas_call(
        functools.partial(_conv_sample_kernel, T=T),
        grid=(1,),
        in_specs=[col(REST_UA), col(REST_UB), col(REST_GC),
                  pl.BlockSpec((None, CONV_STATE, Bd, C_WIDTH), lambda i: (layer, 0, 0, 0)),
                  pl.BlockSpec((CONV_W, C_WIDTH), lambda i: (0, 0)), vec, vec, vec,
                  pl.BlockSpec((C_WIDTH, C_WIDTH), lambda i: (0, 0)), vec],
        out_specs=[pl.BlockSpec((N, C_WIDTH), lambda i: (0, 0)),
                   pl.BlockSpec((CONV_STATE, Bd, C_WIDTH), lambda i: (0, 0, 0))],
        out_shape=[jax.ShapeDtypeStruct((N, C_WIDTH), BF16),
                   jax.ShapeDtypeStruct((CONV_STATE, Bd, C_WIDTH), F32)],
        scratch_shapes=[pltpu.VMEM((LANE_HALVES, N, LANES), F32),
                        pltpu.VMEM((T, Bd, C_WIDTH), F32),
                        pltpu.VMEM((LANE_HALVES, N, LANES), F32)],
        compiler_params=_cparams(("arbitrary",)),
        name="conformer_conv_sample",
    )(rest, rest, rest, state_t, dw_w, dw_b.reshape(1, -1), ln_g.reshape(1, -1), ln_b.reshape(1, -1),
      pw_w, pw_b.reshape(1, -1))


def kernel(x_prompt, x_sample, cache_attn_k, cache_attn_v, state_retention, state_conv, norm_g, w_in, w_out,
           ret_norm_g, conv_dw_w, conv_dw_b, conv_ln_g, conv_ln_b, conv_pw_w, conv_pw_b, final_norm_g):
    Bp, S, D = x_prompt.shape
    Bd, T, _ = x_sample.shape
    depth = w_in.shape[0]
    l_buf = cache_attn_k.shape[2]
    assert S % (ATT_BLK * DIL_GROUPS[-1][1]) == 0 and S % RET_CHUNK == 0 and l_buf == WIN_MAX
    lbuf_p = min(WIN_MAX, S)
    ret_nseq = 128 // T
    ck = jnp.transpose(cache_attn_k, (0, 1, 3, 4, 2))
    cv = jnp.transpose(cache_attn_v, (0, 1, 3, 4, 2))
    conv_state_t = jnp.transpose(state_conv, (0, 2, 1, 3))

    w_in_b = w_in.astype(BF16)
    w_out_b = w_out.astype(BF16)
    pw_b16 = conv_pw_w.astype(BF16)
    bias_p = _prompt_bias()
    bias_c, bias_n = _sample_bias(l_buf, T)
    ret_consts_p = _retention_consts(RET_CHUNK, 1)
    ret_consts_s = _retention_consts(T, ret_nseq)
    zero_ret = jnp.zeros((Bp, RET_PAIRS, LANES, LANES), F32)
    zero_conv = jnp.zeros((Bp, CONV_STATE, C_WIDTH), F32)

    xp = x_prompt
    xs = x_sample.reshape(Bd * T, D)
    kp_l, vp_l, ks_l, vs_l, rp_l, rs_l, cp_l, cs_l = [], [], [], [], [], [], [], []
    for l in range(depth):
        last = l == depth - 1
        conv_p = (conv_dw_w[l], conv_dw_b[l], conv_ln_g[l], conv_ln_b[l], pw_b16[l], conv_pw_b[l])
        qkv_groups, kf, vf, rest = _inproj_prompt(xp, norm_g[l], w_in_b[l])
        ma = _attn_prompt(qkv_groups, rest, bias_p)
        mb, rp = _retention_prompt(rest, zero_ret, ret_consts_p, ret_norm_g[l], RET_CHUNK)
        mc, cp = _conv(rest, zero_conv, *conv_p, tile=256)
        xp = _outproj(xp.reshape(Bp * S, D), ma.reshape(Bp * S, -1), mb.reshape(Bp * S, -1),
                      mc.reshape(Bp * S, -1), w_out_b[l], final_norm_g, last, tm=512).reshape(Bp, S, D)
        kp_l.append(kf[:, S - lbuf_p:].reshape(Bp, lbuf_p, A_HEADS, A_HEAD_DIM))
        vp_l.append(vf[:, S - lbuf_p:].reshape(Bp, lbuf_p, A_HEADS, A_HEAD_DIM))
        rp_l.append(_pairs_to_state(rp))
        cp_l.append(cp)
        q, kf, vf, rest = _inproj_sample(xs, norm_g[l], w_in_b[l])
        rest3 = rest.reshape(Bd, T, REST_WIDTH)
        tok = lambda a: a.reshape(Bd, T, A_WIDTH)
        ma = _attn_sample(tok(q), tok(kf), tok(vf), ck, cv, l, bias_c, bias_n, rest3)
        mb, rs = _retention_sample(rest, _state_to_pairs(state_retention[l]), ret_consts_s, ret_norm_g[l],
                                   T, ret_nseq)
        mc, cs = _conv_sample(rest, conv_state_t, l, *conv_p, T=T)
        xs = _outproj(xs, ma.reshape(Bd * T, -1), mb, mc, w_out_b[l], final_norm_g, last, tm=256)
        ks_l.append(kf.reshape(Bd, T, A_HEADS, A_HEAD_DIM))
        vs_l.append(vf.reshape(Bd, T, A_HEADS, A_HEAD_DIM))
        rs_l.append(_pairs_to_state(rs))
        cs_l.append(cs)
    return (xp, xs.reshape(Bd, T, D),
            jnp.stack(kp_l), jnp.stack(vp_l), jnp.stack(ks_l), jnp.stack(vs_l),
            jnp.stack(rp_l), jnp.stack(rs_l), jnp.stack(cp_l), jnp.transpose(jnp.stack(cs_l), (0, 2, 1, 3)))
```

```python
import functools

import jax
import jax.numpy as jnp
from jax import lax
from jax.experimental import pallas as pl
from jax.experimental.pallas import tpu as pltpu

F32 = jnp.float32
BF16 = jnp.bfloat16

D_MODEL = 1024
A_HEADS = 8
A_HEAD_DIM = 64
A_WIDTH = A_HEADS * A_HEAD_DIM
DIL_GROUPS = ((128, 1), (512, 4), (2048, 16))
WIN_MAX = 2048
ATT_BLK = 128
B_HEADS = 4
B_DIM = 64
B_WIDTH = B_HEADS * B_DIM
RET_CHUNK = 128
C_WIDTH = 256
CONV_W = 31
IN_WIDTH = 4 * A_WIDTH + 4 * B_WIDTH + 3 * C_WIDTH
QKV_WIDTH = 3 * A_WIDTH
REST_WIDTH = IN_WIDTH - QKV_WIDTH
EPS = 1e-6
NEG = -1e30
LANES = 128
HEAD_PAIRS = A_HEADS // 2
RET_PAIRS = B_HEADS // 2
VMEM_LIMIT = 48 * 1024 * 1024

REST_GA = 0
REST_QB = 4
REST_KB = 6
REST_VB = 8
REST_GB = 10
REST_UA = 6
REST_UB = 7
REST_GC = 8

CONTRACT_LAST = (((1,), (1,)), ((), ()))
CONTRACT_FIRST = (((0,), (0,)), ((), ()))


def _cparams(semantics):
    return pltpu.CompilerParams(dimension_semantics=semantics, vmem_limit_bytes=VMEM_LIMIT)


def _silu(x):
    return x * jax.nn.sigmoid(x)


INPROJ_ROWS = 256


def _inproj_body(x_ref, g_ref, w_ref, rows=slice(None)):
    x = x_ref[rows, :]
    h = x * lax.rsqrt(jnp.mean(x * x, axis=-1, keepdims=True) + EPS) * g_ref[...]
    h = h.astype(BF16)

    def cols(c0, c1):
        return jnp.dot(h, w_ref[:, c0:c1], preferred_element_type=F32)

    return cols


def _inproj_prompt_kernel(x_ref, g_ref, w_ref, *refs):
    qkv_refs = (refs[0:3], refs[3:6], refs[6:9])
    kf_ref, vf_ref, rest_ref, stage_ref = refs[9:13]
    tm = x_ref.shape[0]
    rc = INPROJ_ROWS
    for r0 in range(0, tm, rc):
        rows = slice(r0, r0 + rc)
        cols = _inproj_body(x_ref, g_ref, w_ref, rows)
        for t in range(3):
            for hp0 in range(0, HEAD_PAIRS, 2):
                c0 = t * A_WIDTH + hp0 * LANES
                val = cols(c0, c0 + 2 * LANES)
                if t == 0:
                    val = val * (A_HEAD_DIM ** -0.5)
                else:
                    (kf_ref, vf_ref)[t - 1][rows, hp0 * LANES:(hp0 + 2) * LANES] = val
                for hp in (hp0, hp0 + 1):
                    piece = val[:, (hp - hp0) * LANES:(hp - hp0 + 1) * LANES]
                    slab = t * HEAD_PAIRS + hp
                    stage_ref[slab, rows, :] = piece
                    for g, (_, r) in enumerate(DIL_GROUPS):
                        out_ref = qkv_refs[t][g]
                        if r == 1:
                            out_ref[hp, 0, rows, :] = piece.astype(BF16)
                        else:
                            for c in range(r):
                                sub = stage_ref[slab, pl.ds(r0 + c, rc // r, stride=r), :]
                                out_ref[hp, c, r0 // r:(r0 + rc) // r, :] = sub.astype(BF16)
        for c0 in range(0, REST_WIDTH, 2 * LANES):
            rest_ref[rows, c0:c0 + 2 * LANES] = cols(QKV_WIDTH + c0, QKV_WIDTH + c0 + 2 * LANES)


def _inproj_prompt(x, g, w, tm=512):
    B, S, D = x.shape
    nb = S // tm
    row = lambda width: pl.BlockSpec((None, tm, width), lambda b, i: (b, i, 0))
    grp_shapes, grp_specs = [], []
    for _ in range(3):
        for _, r in DIL_GROUPS:
            grp_shapes.append(jax.ShapeDtypeStruct((B, HEAD_PAIRS, r, S // r, LANES), BF16))
            grp_specs.append(pl.BlockSpec((None, HEAD_PAIRS, r, tm // r, LANES), lambda b, i: (b, 0, 0, i, 0)))
    outs = pl.pallas_call(
        _inproj_prompt_kernel,
        grid=(B, nb),
        in_specs=[row(D),
                  pl.BlockSpec((1, D), lambda b, i: (0, 0)),
                  pl.BlockSpec((D, IN_WIDTH), lambda b, i: (0, 0))],
        out_specs=grp_specs + [row(A_WIDTH), row(A_WIDTH), row(REST_WIDTH)],
        out_shape=grp_shapes + [jax.ShapeDtypeStruct((B, S, A_WIDTH), F32),
                                jax.ShapeDtypeStruct((B, S, A_WIDTH), F32),
                                jax.ShapeDtypeStruct((B, S, REST_WIDTH), F32)],
        scratch_shapes=[pltpu.VMEM((3 * HEAD_PAIRS, tm, LANES), F32)],
        compiler_params=_cparams(("arbitrary", "arbitrary")),
        name="inproj_prompt",
    )(x, g.reshape(1, D), w)
    return outs[0:9], outs[9], outs[10], outs[11]


def _inproj_sample_kernel(x_ref, g_ref, w_ref, q_ref, kf_ref, vf_ref, rest_ref):
    cols = _inproj_body(x_ref, g_ref, w_ref)
    q_ref[...] = cols(0, A_WIDTH) * (A_HEAD_DIM ** -0.5)
    kf_ref[...] = cols(A_WIDTH, 2 * A_WIDTH)
    vf_ref[...] = cols(2 * A_WIDTH, 3 * A_WIDTH)
    for c0 in range(0, REST_WIDTH, 768):
        rest_ref[:, c0:c0 + 768] = cols(QKV_WIDTH + c0, QKV_WIDTH + c0 + 768)


def _inproj_sample(x, g, w, tm=256):
    N, D = x.shape
    row = lambda width: pl.BlockSpec((tm, width), lambda i: (i, 0))
    return pl.pallas_call(
        _inproj_sample_kernel,
        grid=(N // tm,),
        in_specs=[row(D),
                  pl.BlockSpec((1, D), lambda i: (0, 0)),
                  pl.BlockSpec((D, IN_WIDTH), lambda i: (0, 0))],
        out_specs=[row(A_WIDTH), row(A_WIDTH), row(A_WIDTH), row(REST_WIDTH)],
        out_shape=[jax.ShapeDtypeStruct((N, A_WIDTH), F32)] * 3
                  + [jax.ShapeDtypeStruct((N, REST_WIDTH), F32)],
        compiler_params=_cparams(("arbitrary",)),
        name="inproj_sample",
    )(x, g.reshape(1, D), w)


def _outproj_kernel(x_ref, ma_ref, mb_ref, mc_ref, w_ref, g_ref, o_ref, *, final_norm):
    y = x_ref[...]
    y = y + jnp.dot(ma_ref[...], w_ref[0:A_WIDTH, :], preferred_element_type=F32)
    y = y + jnp.dot(mb_ref[...], w_ref[A_WIDTH:A_WIDTH + B_WIDTH, :], preferred_element_type=F32)
    y = y + jnp.dot(mc_ref[...], w_ref[A_WIDTH + B_WIDTH:, :], preferred_element_type=F32)
    if final_norm:
        y = y * lax.rsqrt(jnp.mean(y * y, axis=-1, keepdims=True) + EPS) * g_ref[...]
    o_ref[...] = y


def _outproj(x, ma, mb, mc, w, g_final, final_norm, tm):
    N, D = x.shape
    row = lambda width: pl.BlockSpec((tm, width), lambda i: (i, 0))
    return pl.pallas_call(
        functools.partial(_outproj_kernel, final_norm=final_norm),
        grid=(N // tm,),
        in_specs=[row(D), row(A_WIDTH), row(B_WIDTH), row(C_WIDTH),
                  pl.BlockSpec((D, D), lambda i: (0, 0)),
                  pl.BlockSpec((1, D), lambda i: (0, 0))],
        out_specs=row(D),
        out_shape=jax.ShapeDtypeStruct((N, D), F32),
        compiler_params=_cparams(("arbitrary",)),
        name="outproj",
    )(x, ma, mb, mc, w, g_final.reshape(1, D))


def _alibi_slopes():
    return jnp.exp2(-8.0 * (jnp.arange(A_HEADS, dtype=F32) + 1.0) / A_HEADS)


def _prompt_bias():
    slopes = _alibi_slopes()
    qi = jnp.arange(ATT_BLK)[:, None]
    ki = jnp.arange(2 * ATT_BLK)[None, :] - ATT_BLK
    dist = qi - ki
    tiles = []
    for w, r in DIL_GROUPS:
        n_back = w // r
        valid = (dist >= 0) & (dist <= n_back)
        pen = (slopes * r)[:, None, None] * dist.astype(F32)[None]
        b = jnp.where(valid[None], -pen, NEG)
        tiles.append(b.reshape(HEAD_PAIRS, 2 * ATT_BLK, 2 * ATT_BLK))
    normal = jnp.stack(tiles)
    first = jnp.concatenate([normal[..., ATT_BLK:], jnp.full_like(normal[..., ATT_BLK:], NEG)], axis=-1)
    return jnp.stack([normal, first])


def _attn_unit(q, kwin, vwin, bias, lo_half):
    zero = jnp.zeros_like(q)
    q2 = jnp.concatenate([jnp.where(lo_half, q, zero), jnp.where(lo_half, zero, q)], axis=0)
    s = lax.dot_general(q2, kwin, CONTRACT_LAST, preferred_element_type=F32) + bias
    m = jnp.max(s, axis=-1, keepdims=True)
    p = jnp.exp(s - m)
    den = jnp.sum(p, axis=-1, keepdims=True)
    o = jnp.dot(p.astype(BF16), vwin, preferred_element_type=F32) / den
    lse = m + jnp.log(den)
    o_pair = jnp.where(lo_half, o[:ATT_BLK], o[ATT_BLK:])
    lse_pair = jnp.where(lo_half, lse[:ATT_BLK], lse[ATT_BLK:])
    return o_pair, lse_pair


ATTN_UNROLL = 8


def _attn_prompt_group(g, q_ref, k_ref, v_ref, bias_ref, o_ref, l_ref):
    r = DIL_GROUPS[g][1]
    S = o_ref.shape[0]
    nblk = S // r // ATT_BLK
    lo_half = lax.broadcasted_iota(jnp.int32, (ATT_BLK, LANES), 1) < A_HEAD_DIM

    def trip(it, carry):
        for u in range(ATTN_UNROLL):
            n = it * ATTN_UNROLL + u
            blk, c = (n, 0) if r == 1 else (lax.div(n, jnp.int32(r)), lax.rem(n, jnp.int32(r)))
            q0 = pl.multiple_of(blk * ATT_BLK, ATT_BLK)
            q = q_ref[c, pl.ds(q0, ATT_BLK), :]
            if nblk == 1:
                kwin, vwin = k_ref[c], v_ref[c]
                bias = bias_ref[0, g, :, ATT_BLK:]
            else:
                k0 = pl.multiple_of(jnp.maximum(blk - 1, 0) * ATT_BLK, ATT_BLK)
                kwin = k_ref[c, pl.ds(k0, 2 * ATT_BLK), :]
                vwin = v_ref[c, pl.ds(k0, 2 * ATT_BLK), :]
                bias = bias_ref[jnp.where(blk == 0, 1, 0), g]
            o_pair, lse_pair = _attn_unit(q, kwin, vwin, bias, lo_half)
            rows = pl.ds(q0, ATT_BLK) if r == 1 else pl.ds(blk * (ATT_BLK * r) + c, ATT_BLK, stride=r)
            o_ref[rows, :] = o_pair
            l_ref[rows, :] = lse_pair
        return carry

    lax.fori_loop(0, (r * nblk) // ATTN_UNROLL, trip, 0)


def _attn_prompt_merge(o_refs, l_refs, ga_ref, out_ref):
    o1_ref, o4_ref, o16_ref = o_refs
    l1_ref, l4_ref, l16_ref = l_refs
    S = out_ref.shape[0]
    rows_per = 256

    def merge(i, carry):
        rows = pl.ds(pl.multiple_of(i * rows_per, rows_per), rows_per)
        l1, l4, l16 = l1_ref[rows, :], l4_ref[rows, :], l16_ref[rows, :]
        mx = jnp.maximum(jnp.maximum(l1, l4), l16)
        e1, e4, e16 = jnp.exp(l1 - mx), jnp.exp(l4 - mx), jnp.exp(l16 - mx)
        tot = e1 + e4 + e16
        y = (e1 / tot) * o1_ref[rows, :] + (e4 / tot) * o4_ref[rows, :] + (e16 / tot) * o16_ref[rows, :]
        out_ref[rows, :] = (y * _silu(ga_ref[rows, :])).astype(BF16)
        return carry

    lax.fori_loop(0, S // rows_per, merge, 0)


def _sample_bias(l_buf, T):
    slopes = _alibi_slopes()
    t = jnp.arange(T)[:, None]
    rho = jnp.arange(l_buf)[None, :]
    tn = jnp.arange(LANES)[None, :]
    cache, new = [], []
    for w, r in DIL_GROUPS:
        n_back = w // r
        sl = (slopes * r)[:, None, None]
        dc = l_buf + t - rho
        jc = dc // r
        vc = (dc % r == 0) & (jc <= n_back)
        cache.append(jnp.where(vc[None], -(sl * jc.astype(F32)[None]), NEG).reshape(A_HEADS * T, l_buf))
        dn = t - tn
        jn = dn // r
        vn = (dn >= 0) & (dn % r == 0) & (jn <= n_back)
        new.append(jnp.where(vn[None], -(sl * jn.astype(F32)[None]), NEG).reshape(A_HEADS * T, LANES))
    return jnp.stack(cache), jnp.stack(new)


def _attn_sample_kernel(q_ref, kn_ref, vn_ref, kc_ref, vc_ref, bc_ref, bn_ref, ga_ref, out_ref):
    T = q_ref.shape[0]
    l_buf = kc_ref.shape[-1]
    lo_half = lax.broadcasted_iota(jnp.int32, (T, LANES), 1) < A_HEAD_DIM
    pad = jnp.zeros((LANES - T, A_WIDTH), F32)
    kn = jnp.concatenate([kn_ref[...], pad], axis=0).astype(BF16)
    vn = jnp.concatenate([vn_ref[...], pad], axis=0).astype(BF16)
    q = q_ref[...]
    sc_parts, sn_parts = [], []
    for hp in range(HEAD_PAIRS):
        lanes = slice(hp * LANES, (hp + 1) * LANES)
        qp = q[:, lanes]
        q2 = jnp.concatenate([jnp.where(lo_half, qp, 0.0), jnp.where(lo_half, 0.0, qp)], axis=0).astype(BF16)
        kt = kc_ref[2 * hp:2 * hp + 2].reshape(LANES, l_buf).astype(BF16)
        sc_parts.append(jnp.dot(q2, kt, preferred_element_type=F32))
        sn_parts.append(lax.dot_general(q2, kn[:, lanes], CONTRACT_LAST, preferred_element_type=F32))
    sc = jnp.concatenate(sc_parts, axis=0)
    sn = jnp.concatenate(sn_parts, axis=0)
    starts = [l_buf - min(w, l_buf) for w, _ in DIL_GROUPS]
    pcs, pns, lses, dens = [], [], [], []
    for g, c0 in enumerate(starts):
        scg = sc[:, c0:] + bc_ref[g, :, c0:]
        sng = sn + bn_ref[g]
        m = jnp.maximum(jnp.max(scg, axis=-1, keepdims=True), jnp.max(sng, axis=-1, keepdims=True))
        pc = jnp.exp(scg - m)
        pn = jnp.exp(sng - m)
        den = jnp.sum(pc, axis=-1, keepdims=True) + jnp.sum(pn, axis=-1, keepdims=True)
        pcs.append(pc)
        pns.append(pn)
        dens.append(den)
        lses.append(m + jnp.log(den))
    mx = jnp.maximum(jnp.maximum(lses[0], lses[1]), lses[2])
    es = [jnp.exp(l - mx) for l in lses]
    tot = es[0] + es[1] + es[2]
    coef = [(e / tot) / d for e, d in zip(es, dens)]
    edges = sorted(set(starts)) + [l_buf]
    segs = []
    for lo, hi in zip(edges[:-1], edges[1:]):
        terms = [coef[g] * pcs[g][:, lo - c0:hi - c0] for g, c0 in enumerate(starts) if c0 <= lo]
        segs.append(functools.reduce(lambda a, b: a + b, terms))
    pc = jnp.concatenate(segs, axis=1).astype(BF16)
    pn = (coef[0] * pns[0] + coef[1] * pns[1] + coef[2] * pns[2]).astype(BF16)
    ys = []
    for hp in range(HEAD_PAIRS):
        lanes = slice(hp * LANES, (hp + 1) * LANES)
        rows = slice(2 * T * hp, 2 * T * (hp + 1))
        vt = vc_ref[2 * hp:2 * hp + 2].reshape(LANES, l_buf).astype(BF16)
        o = lax.dot_general(pc[rows], vt, CONTRACT_LAST, preferred_element_type=F32)
        o = o + jnp.dot(pn[rows], vn[:, lanes], preferred_element_type=F32)
        ys.append(jnp.where(lo_half, o[:T], o[T:]))
    y = jnp.concatenate(ys, axis=1)
    out_ref[...] = (y * _silu(ga_ref[...])).astype(BF16)


N_PHASES = len(DIL_GROUPS) + 1


def _attn_fused_kernel(*refs):
    n_groups = len(DIL_GROUPS)
    qkv = refs[0:3 * n_groups]
    bias_ref, ga_ref = refs[3 * n_groups:3 * n_groups + 2]
    sample_in = refs[3 * n_groups + 2:3 * n_groups + 10]
    out_p_ref, out_s_ref = refs[3 * n_groups + 10:3 * n_groups + 12]
    scratch = refs[3 * n_groups + 12:]
    o_refs, l_refs = scratch[0:n_groups], scratch[n_groups:2 * n_groups]
    j = pl.program_id(2)

    for g in range(n_groups):
        @pl.when(j == g)
        def _(g=g):
            _attn_prompt_group(g, qkv[g], qkv[n_groups + g], qkv[2 * n_groups + g], bias_ref, o_refs[g], l_refs[g])

    @pl.when(j == n_groups)
    def _():
        _attn_prompt_merge(o_refs, l_refs, ga_ref, out_p_ref)

    _attn_sample_kernel(*sample_in, out_s_ref)


def _attn_fused(qkv_groups, rest_p, bias_p, q, kn, vn, cache_kt, cache_vt, layer, bias_c, bias_n, rest_s):
    B, S, _ = rest_p.shape
    Bd, T, _ = q.shape
    l_buf = cache_kt.shape[-1]
    HT = A_HEADS * T
    G = len(DIL_GROUPS)
    J = Bd // (B * HEAD_PAIRS)
    assert Bd == B * HEAD_PAIRS * J and J >= N_PHASES and (S // ATT_BLK) % ATTN_UNROLL == 0
    seq = lambda b, hp, j: (b * HEAD_PAIRS + hp) * J + j
    ins = list(qkv_groups)
    specs = [pl.BlockSpec((None, None) + t.shape[2:], lambda b, hp, j: (b, hp, 0, 0, 0)) for t in ins]
    tok = pl.BlockSpec((None, T, A_WIDTH), lambda b, hp, j: (seq(b, hp, j), 0, 0))
    cache = pl.BlockSpec((None, None, A_HEADS, A_HEAD_DIM, l_buf), lambda b, hp, j: (layer, seq(b, hp, j), 0, 0, 0))
    ins += [bias_p, rest_p, q, kn, vn, cache_kt, cache_vt, bias_c, bias_n, rest_s]
    specs += [pl.BlockSpec((2, G, None, 2 * ATT_BLK, 2 * ATT_BLK), lambda b, hp, j: (0, 0, hp, 0, 0)),
              pl.BlockSpec((None, S, LANES), lambda b, hp, j: (b, 0, REST_GA + hp)),
              tok, tok, tok, cache, cache,
              pl.BlockSpec((G, HT, l_buf), lambda b, hp, j: (0, 0, 0)),
              pl.BlockSpec((G, HT, LANES), lambda b, hp, j: (0, 0, 0)),
              pl.BlockSpec((None, T, A_WIDTH), lambda b, hp, j: (seq(b, hp, j), 0, REST_GA))]
    return pl.pallas_call(
        _attn_fused_kernel,
        grid=(B, HEAD_PAIRS, J),
        in_specs=specs,
        out_specs=[pl.BlockSpec((None, S, LANES), lambda b, hp, j: (b, 0, hp)), tok],
        out_shape=[jax.ShapeDtypeStruct((B, S, A_WIDTH), BF16),
                   jax.ShapeDtypeStruct((Bd, T, A_WIDTH), BF16)],
        scratch_shapes=[pltpu.VMEM((S, LANES), F32)] * (2 * G),
        compiler_params=_cparams(("arbitrary", "arbitrary", "arbitrary")),
        name="attn_fused",
    )(*ins)


def _retention_consts(chunk, nseq):
    H = B_HEADS
    log_g = jnp.log(1.0 - jnp.exp2(-5.0 - jnp.arange(H, dtype=F32)))
    i = jnp.arange(chunk, dtype=F32)
    diff = i[:, None] - i[None, :]
    dmat = jnp.where(diff >= 0, jnp.exp(jnp.maximum(diff, 0.0)[None] * log_g[:, None, None]), 0.0)
    q_dec = jnp.exp((i[:, None] + 1.0) * log_g[None, :])
    k_dec = jnp.exp((chunk - 1.0 - i)[:, None] * log_g[None, :])
    c_dec = jnp.exp(chunk * log_g)
    rows = nseq * chunk
    eye = jnp.eye(nseq, dtype=F32)
    dmat = (eye[None, :, None, :, None] * dmat[:, None, :, None, :]).reshape(H, rows, rows)
    lanes = lambda t: jnp.repeat(t, B_DIM, axis=-1)
    pair = lambda t: t.reshape(t.shape[0], RET_PAIRS, LANES).transpose(1, 0, 2)
    dmat2 = dmat.reshape(RET_PAIRS, 2 * rows, rows)
    qd = pair(lanes(jnp.tile(q_dec, (nseq, 1))))
    kd = pair(lanes(jnp.tile(k_dec, (nseq, 1))))
    cd = lanes(c_dec[None, :]).reshape(RET_PAIRS, 1, LANES)
    cd = jnp.broadcast_to(cd, (RET_PAIRS, LANES, LANES))
    return dmat2, qd, kd, cd


def _ret_intra(qf, kf, v, dmat, lo_half):
    rows = qf.shape[0]
    q2 = jnp.concatenate([jnp.where(lo_half, qf, 0.0), jnp.where(lo_half, 0.0, qf)], axis=0).astype(BF16)
    att = lax.dot_general(q2, kf.astype(BF16), CONTRACT_LAST, preferred_element_type=F32) * dmat
    oi = jnp.dot(att.astype(BF16), v, preferred_element_type=F32)
    return jnp.where(lo_half, oi[:rows], oi[rows:])


def _ret_norm_gate(o, gn, gate, lo_half):
    def head_sum(x):
        a = jnp.sum(jnp.where(lo_half, x, 0.0), axis=-1, keepdims=True)
        b = jnp.sum(jnp.where(lo_half, 0.0, x), axis=-1, keepdims=True)
        return jnp.where(lo_half, a, b)

    mu = head_sum(o) / B_DIM
    d = o - mu
    var = head_sum(d * d) / B_DIM
    y = d * lax.rsqrt(var + EPS) * gn
    return (y * _silu(gate)).astype(BF16)


def _same_head_mask():
    r2 = lax.broadcasted_iota(jnp.int32, (LANES, LANES), 0) < B_DIM
    c2 = lax.broadcasted_iota(jnp.int32, (LANES, LANES), 1) < B_DIM
    return r2 == c2


RET_UNROLL = 4


def _retention_prompt_kernel(q_ref, k_ref, v_ref, gate_ref, s0_ref, dmat_ref, qd_ref, kd_ref, cd_ref, gn_ref,
                             out_ref, sfin_ref, oi_ref, kv_ref, st_ref, *, chunk):
    L = q_ref.shape[0]
    nc = L // chunk
    lo_half = lax.broadcasted_iota(jnp.int32, (chunk, LANES), 1) < B_DIM
    same_head = _same_head_mask()

    def chunk_rows(ci):
        return pl.ds(pl.multiple_of(ci * chunk, chunk), chunk)

    def phase1(it, carry):
        for u in range(RET_UNROLL):
            ci = it * RET_UNROLL + u
            rows = chunk_rows(ci)
            qf = q_ref[rows, :]
            kf = k_ref[rows, :] * (B_DIM ** -0.5)
            v = v_ref[rows, :].astype(BF16)
            oi_ref[rows, :] = _ret_intra(qf, kf, v, dmat_ref[...], lo_half)
            kv = lax.dot_general((kf * kd_ref[...]).astype(BF16), v, CONTRACT_FIRST, preferred_element_type=F32)
            kv_ref[ci] = jnp.where(same_head, kv, 0.0)
        return carry

    lax.fori_loop(0, nc // RET_UNROLL, phase1, 0)

    def phase2(ci, state):
        st_ref[ci] = state.astype(BF16)
        return state * cd_ref[...] + kv_ref[ci]

    sfin_ref[...] = lax.fori_loop(0, nc, phase2, s0_ref[...])

    def phase3(it, carry):
        for u in range(RET_UNROLL):
            ci = it * RET_UNROLL + u
            rows = chunk_rows(ci)
            inter = jnp.dot(q_ref[rows, :].astype(BF16), st_ref[ci], preferred_element_type=F32) * qd_ref[...]
            out_ref[rows, :] = _ret_norm_gate(oi_ref[rows, :] + inter, gn_ref[...], gate_ref[rows, :], lo_half)
        return carry

    lax.fori_loop(0, nc // RET_UNROLL, phase3, 0)


def _retention_prompt(rest, s0_bd, consts, gn_g, chunk):
    B, L, _ = rest.shape
    dmat2, qd, kd, cd = consts
    col = lambda blk: pl.BlockSpec((None, L, LANES), lambda b, p, blk=blk: (b, 0, blk + p))
    pair_const = lambda a: pl.BlockSpec((None,) + a.shape[1:], lambda b, p: (p,) + (0,) * (a.ndim - 1))
    st = pl.BlockSpec((None, None, LANES, LANES), lambda b, p: (b, p, 0, 0))
    gn = gn_g.reshape(RET_PAIRS, 1, LANES)
    return pl.pallas_call(
        functools.partial(_retention_prompt_kernel, chunk=chunk),
        grid=(B, RET_PAIRS),
        in_specs=[col(REST_QB), col(REST_KB), col(REST_VB), col(REST_GB), st,
                  pair_const(dmat2), pair_const(qd), pair_const(kd), pair_const(cd), pair_const(gn)],
        out_specs=[pl.BlockSpec((None, L, LANES), lambda b, p: (b, 0, p)), st],
        out_shape=[jax.ShapeDtypeStruct((B, L, B_WIDTH), BF16),
                   jax.ShapeDtypeStruct((B, RET_PAIRS, LANES, LANES), F32)],
        scratch_shapes=[pltpu.VMEM((L, LANES), F32),
                        pltpu.VMEM((L // chunk, LANES, LANES), F32),
                        pltpu.VMEM((L // chunk, LANES, LANES), BF16)],
        compiler_params=_cparams(("arbitrary", "arbitrary")),
        name="retention_prompt",
    )(rest, rest, rest, rest, s0_bd, dmat2, qd, kd, cd, gn)


def _retention_sample_kernel(q_ref, k_ref, v_ref, gate_ref, s0_ref, dmat_ref, qd_ref, kd_ref, cd_ref, gn_ref,
                             out_ref, sfin_ref, *, T):
    rows = q_ref.shape[0]
    nseq = rows // T
    lo_half = lax.broadcasted_iota(jnp.int32, (rows, LANES), 1) < B_DIM
    seq_of_row = lax.broadcasted_iota(jnp.int32, (rows, LANES), 0) // T

    def expand(x):
        return jnp.concatenate([jnp.where(seq_of_row == s, x, 0.0) for s in range(nseq)], axis=1).astype(BF16)

    qf = q_ref[...]
    kf = k_ref[...] * (B_DIM ** -0.5)
    v = v_ref[...].astype(BF16)
    s0 = s0_ref[...]
    o = _ret_intra(qf, kf, v, dmat_ref[...], lo_half)
    s0_rows = s0.reshape(nseq * LANES, LANES).astype(BF16)
    o = o + jnp.dot(expand(qf), s0_rows, preferred_element_type=F32) * qd_ref[...]
    kv = lax.dot_general(expand(kf * kd_ref[...]), v, CONTRACT_FIRST, preferred_element_type=F32)
    out_ref[...] = _ret_norm_gate(o, gn_ref[...], gate_ref[...], lo_half)
    sfin_ref[...] = s0 * cd_ref[...][None] + jnp.where(_same_head_mask()[None],
                                                      kv.reshape(nseq, LANES, LANES), 0.0)


def _retention_sample(rest, s0_bd, consts, gn_g, T, nseq=16):
    N, _ = rest.shape
    rows = nseq * T
    dmat2, qd, kd, cd = consts
    col = lambda blk: pl.BlockSpec((rows, LANES), lambda i, p, blk=blk: (i, blk + p))
    pair_const = lambda a: pl.BlockSpec((None,) + a.shape[1:], lambda i, p: (p,) + (0,) * (a.ndim - 1))
    st = pl.BlockSpec((nseq, None, LANES, LANES), lambda i, p: (i, p, 0, 0))
    gn = gn_g.reshape(RET_PAIRS, 1, LANES)
    return pl.pallas_call(
        functools.partial(_retention_sample_kernel, T=T),
        grid=(N // rows, RET_PAIRS),
        in_specs=[col(REST_QB), col(REST_KB), col(REST_VB), col(REST_GB), st,
                  pair_const(dmat2), pair_const(qd), pair_const(kd), pair_const(cd), pair_const(gn)],
        out_specs=[pl.BlockSpec((rows, LANES), lambda i, p: (i, p)), st],
        out_shape=[jax.ShapeDtypeStruct((N, B_WIDTH), BF16),
                   jax.ShapeDtypeStruct(s0_bd.shape, F32)],
        compiler_params=_cparams(("arbitrary", "arbitrary")),
        name="retention_sample",
    )(rest, rest, rest, rest, s0_bd, dmat2, qd, kd, cd, gn)


def _state_to_pairs(s):
    B = s.shape[0]
    s = s.reshape(B, RET_PAIRS, 2, B_DIM, B_DIM)
    z = jnp.zeros_like(s[:, :, 0])
    top = jnp.concatenate([s[:, :, 0], z], axis=-1)
    bot = jnp.concatenate([z, s[:, :, 1]], axis=-1)
    return jnp.concatenate([top, bot], axis=-2)


def _pairs_to_state(sbd):
    B = sbd.shape[0]
    a = sbd[:, :, :B_DIM, :B_DIM]
    b = sbd[:, :, B_DIM:, B_DIM:]
    return jnp.stack([a, b], axis=2).reshape(B, B_HEADS, B_DIM, B_DIM)


CONV_STATE = CONV_W - 1
CONV_PAD = 32
LANE_HALVES = C_WIDTH // LANES


def _conv_kernel(ua_ref, ub_ref, gc_ref, s0_ref, dww_ref, dwb_ref, lng_ref, lnb_ref, pww_ref, pwb_ref,
                 out_ref, sfin_ref, ext_ref, *, tile):
    L = ua_ref.shape[0]
    u = ua_ref[...] * jax.nn.sigmoid(ub_ref[...])
    s0 = s0_ref[...]
    for h in range(LANE_HALVES):
        lanes = slice(h * LANES, (h + 1) * LANES)
        ext_ref[h, 0:CONV_PAD, :] = s0[:, lanes]
        ext_ref[h, CONV_PAD:CONV_PAD + L, :] = u[:, lanes]
    for h in range(LANE_HALVES):
        sfin_ref[:, h * LANES:(h + 1) * LANES] = ext_ref[h, L:L + CONV_PAD, :]

    def step(i, carry):
        t0 = pl.multiple_of(i * tile, tile)
        halves = []
        for h in range(LANE_HALVES):
            acc = jnp.zeros((tile, LANES), F32)
            for w in range(CONV_W):
                rows = pl.ds(t0 + (CONV_PAD - CONV_STATE + w), tile, stride=1)
                acc = acc + ext_ref[h, rows, :] * dww_ref[w:w + 1, h * LANES:(h + 1) * LANES]
            halves.append(acc)
        y = jnp.concatenate(halves, axis=1) + dwb_ref[...]
        mu = jnp.mean(y, axis=-1, keepdims=True)
        d = y - mu
        var = jnp.mean(d * d, axis=-1, keepdims=True)
        yn = d * lax.rsqrt(var + EPS) * lng_ref[...] + lnb_ref[...]
        yn = _silu(yn).astype(BF16)
        z = jnp.dot(yn, pww_ref[...], preferred_element_type=F32) + pwb_ref[...]
        rows = pl.ds(t0, tile)
        out_ref[rows, :] = (z * _silu(gc_ref[rows, :])).astype(BF16)
        return carry

    lax.fori_loop(0, L // tile, step, 0)


def _conv(rest, s0, dw_w, dw_b, ln_g, ln_b, pw_w, pw_b, tile):
    B, L, _ = rest.shape
    col = lambda blk: pl.BlockSpec((None, L, C_WIDTH), lambda b, blk=blk: (b, 0, blk))
    st = pl.BlockSpec((None, CONV_PAD, C_WIDTH), lambda b: (b, 0, 0))
    vec = pl.BlockSpec((1, C_WIDTH), lambda b: (0, 0))
    s0 = jnp.pad(s0, ((0, 0), (CONV_PAD - CONV_STATE, 0), (0, 0)))
    out, sfin = pl.pallas_call(
        functools.partial(_conv_kernel, tile=tile),
        grid=(B,),
        in_specs=[col(REST_UA), col(REST_UB), col(REST_GC), st,
                  pl.BlockSpec((CONV_W, C_WIDTH), lambda b: (0, 0)), vec, vec, vec,
                  pl.BlockSpec((C_WIDTH, C_WIDTH), lambda b: (0, 0)), vec],
        out_specs=[pl.BlockSpec((None, L, C_WIDTH), lambda b: (b, 0, 0)), st],
        out_shape=[jax.ShapeDtypeStruct((B, L, C_WIDTH), BF16),
                   jax.ShapeDtypeStruct((B, CONV_PAD, C_WIDTH), F32)],
        scratch_shapes=[pltpu.VMEM((LANE_HALVES, CONV_PAD + L, LANES), F32)],
        compiler_params=_cparams(("arbitrary",)),
        name="conformer_conv",
    )(rest, rest, rest, s0, dw_w, dw_b.reshape(1, -1), ln_g.reshape(1, -1), ln_b.reshape(1, -1),
      pw_w, pw_b.reshape(1, -1))
    return out, sfin[:, CONV_PAD - CONV_STATE:]


def _conv_sample_kernel(ua_ref, ub_ref, gc_ref, s0_ref, dww_ref, dwb_ref, lng_ref, lnb_ref, pww_ref, pwb_ref,
                        out_ref, sfin_ref, u_ref, ut_ref, z_ref, *, T):
    N = ua_ref.shape[0]
    Bd = N // T
    u = ua_ref[...] * jax.nn.sigmoid(ub_ref[...])
    for h in range(LANE_HALVES):
        u_ref[h] = u[:, h * LANES:(h + 1) * LANES]
    for t in range(T):
        ut_ref[t] = jnp.concatenate([u_ref[h, pl.ds(t, Bd, stride=T), :] for h in range(LANE_HALVES)], axis=1)

    def ext(tau):
        return s0_ref[tau] if tau < CONV_STATE else ut_ref[tau - CONV_STATE]

    for tau in range(CONV_STATE):
        sfin_ref[tau] = ext(tau + T)
    for t in range(T):
        acc = jnp.zeros((Bd, C_WIDTH), F32)
        for w in range(CONV_W):
            acc = acc + ext(t + w) * dww_ref[w:w + 1, :]
        y = acc + dwb_ref[...]
        mu = jnp.mean(y, axis=-1, keepdims=True)
        d = y - mu
        var = jnp.mean(d * d, axis=-1, keepdims=True)
        yn = d * lax.rsqrt(var + EPS) * lng_ref[...] + lnb_ref[...]
        yn = _silu(yn).astype(BF16)
        z = jnp.dot(yn, pww_ref[...], preferred_element_type=F32) + pwb_ref[...]
        for h in range(LANE_HALVES):
            z_ref[h, pl.ds(t, Bd, stride=T), :] = z[:, h * LANES:(h + 1) * LANES]
    z = jnp.concatenate([z_ref[h] for h in range(LANE_HALVES)], axis=1)
    out_ref[...] = (z * _silu(gc_ref[...])).astype(BF16)


def _conv_sample(rest, state_t, layer, dw_w, dw_b, ln_g, ln_b, pw_w, pw_b, T):
    N, _ = rest.shape
    Bd = N // T
    col = lambda blk: pl.BlockSpec((N, C_WIDTH), lambda i, blk=blk: (0, blk))
    vec = pl.BlockSpec((1, C_WIDTH), lambda i: (0, 0))
    return pl.pallas_call(
        functools.partial(_conv_sample_kernel, T=T),
        grid=(1,),
        in_specs=[col(REST_UA), col(REST_UB), col(REST_GC),
                  pl.BlockSpec((None, CONV_STATE, Bd, C_WIDTH), lambda i: (layer, 0, 0, 0)),
                  pl.BlockSpec((CONV_W, C_WIDTH), lambda i: (0, 0)), vec, vec, vec,
                  pl.BlockSpec((C_WIDTH, C_WIDTH), lambda i: (0, 0)), vec],
        out_specs=[pl.BlockSpec((N, C_WIDTH), lambda i: (0, 0)),
                   pl.BlockSpec((CONV_STATE, Bd, C_WIDTH), lambda i: (0, 0, 0))],
        out_shape=[jax.ShapeDtypeStruct((N, C_WIDTH), BF16),
                   jax.ShapeDtypeStruct((CONV_STATE, Bd, C_WIDTH), F32)],
        scratch_shapes=[pltpu.VMEM((LANE_HALVES, N, LANES), F32),
                        pltpu.VMEM((T, Bd, C_WIDTH), F32),
                        pltpu.VMEM((LANE_HALVES, N, LANES), F32)],
        compiler_params=_cparams(("arbitrary",)),
        name="conformer_conv_sample",
    )(rest, rest, rest, state_t, dw_w, dw_b.reshape(1, -1), ln_g.reshape(1, -1), ln_b.reshape(1, -1),
      pw_w, pw_b.reshape(1, -1))


def kernel(x_prompt, x_sample, cache_attn_k, cache_attn_v, state_retention, state_conv, norm_g, w_in, w_out,
           ret_norm_g, conv_dw_w, conv_dw_b, conv_ln_g, conv_ln_b, conv_pw_w, conv_pw_b, final_norm_g):
    Bp, S, D = x_prompt.shape
    Bd, T, _ = x_sample.shape
    depth = w_in.shape[0]
    l_buf = cache_attn_k.shape[2]
    assert S % (ATT_BLK * DIL_GROUPS[-1][1]) == 0 and S % RET_CHUNK == 0 and l_buf == WIN_MAX
    lbuf_p = min(WIN_MAX, S)
    ret_nseq = 128 // T
    ck = jnp.transpose(cache_attn_k, (0, 1, 3, 4, 2))
    cv = jnp.transpose(cache_attn_v, (0, 1, 3, 4, 2))
    conv_state_t = jnp.transpose(state_conv, (0, 2, 1, 3))

    w_in_b = w_in.astype(BF16)
    w_out_b = w_out.astype(BF16)
    pw_b16 = conv_pw_w.astype(BF16)
    bias_p = _prompt_bias()
    bias_c, bias_n = _sample_bias(l_buf, T)
    ret_consts_p = _retention_consts(RET_CHUNK, 1)
    ret_consts_s = _retention_consts(T, ret_nseq)
    zero_ret = jnp.zeros((Bp, RET_PAIRS, LANES, LANES), F32)
    zero_conv = jnp.zeros((Bp, CONV_STATE, C_WIDTH), F32)

    xp = x_prompt
    xs = x_sample.reshape(Bd * T, D)
    kp_l, vp_l, ks_l, vs_l, rp_l, rs_l, cp_l, cs_l = [], [], [], [], [], [], [], []
    for l in range(depth):
        last = l == depth - 1
        conv_p = (conv_dw_w[l], conv_dw_b[l], conv_ln_g[l], conv_ln_b[l], pw_b16[l], conv_pw_b[l])
        qkv_groups, kf, vf, rest = _inproj_prompt(xp, norm_g[l], w_in_b[l])
        q_s, kf_s, vf_s, rest_s = _inproj_sample(xs, norm_g[l], w_in_b[l])
        tok = lambda a: a.reshape(Bd, T, -1)
        ma, ma_s = _attn_fused(qkv_groups, rest, bias_p, tok(q_s), tok(kf_s), tok(vf_s), ck, cv, l,
                               bias_c, bias_n, tok(rest_s))
        mb, rp = _retention_prompt(rest, zero_ret, ret_consts_p, ret_norm_g[l], RET_CHUNK)
        mc, cp = _conv(rest, zero_conv, *conv_p, tile=512)
        xp = _outproj(xp.reshape(Bp * S, D), ma.reshape(Bp * S, -1), mb.reshape(Bp * S, -1),
                      mc.reshape(Bp * S, -1), w_out_b[l], final_norm_g, last, tm=512).reshape(Bp, S, D)
        kp_l.append(kf[:, S - lbuf_p:].reshape(Bp, lbuf_p, A_HEADS, A_HEAD_DIM))
        vp_l.append(vf[:, S - lbuf_p:].reshape(Bp, lbuf_p, A_HEADS, A_HEAD_DIM))
        rp_l.append(_pairs_to_state(rp))
        cp_l.append(cp)
        mb, rs = _retention_sample(rest_s, _state_to_pairs(state_retention[l]), ret_consts_s, ret_norm_g[l],
                                   T, ret_nseq)
        mc, cs = _conv_sample(rest_s, conv_state_t, l, *conv_p, T=T)
        xs = _outproj(xs, ma_s.reshape(Bd * T, -1), mb, mc, w_out_b[l], final_norm_g, last, tm=256)
        ks_l.append(kf_s.reshape(Bd, T, A_HEADS, A_HEAD_DIM))
        vs_l.append(vf_s.reshape(Bd, T, A_HEADS, A_HEAD_DIM))
        rs_l.append(_pairs_to_state(rs))
        cs_l.append(cs)
    return (xp, xs.reshape(Bd, T, D),
            jnp.stack(kp_l), jnp.stack(vp_l), jnp.stack(ks_l), jnp.stack(vs_l),
            jnp.stack(rp_l), jnp.stack(rs_l), jnp.stack(cp_l), jnp.transpose(jnp.stack(cs_l), (0, 2, 1, 3)))
```

```python
import functools

import jax
import jax.numpy as jnp
import numpy as np
from jax import lax
from jax.experimental import pallas as pl
from jax.experimental.pallas import tpu as pltpu

F32 = jnp.float32
BF16 = jnp.bfloat16

D_MODEL = 1024
A_HEADS = 8
A_HEAD_DIM = 64
A_WIDTH = A_HEADS * A_HEAD_DIM
DIL_GROUPS = ((128, 1), (512, 4), (2048, 16))
WIN_MAX = 2048
ATT_BLK = 128
B_HEADS = 4
B_DIM = 64
B_WIDTH = B_HEADS * B_DIM
RET_CHUNK = 128
C_WIDTH = 256
CONV_W = 31
IN_WIDTH = 4 * A_WIDTH + 4 * B_WIDTH + 3 * C_WIDTH
QKV_WIDTH = 3 * A_WIDTH
REST_WIDTH = IN_WIDTH - QKV_WIDTH
EPS = 1e-6
NEG = -1e30
LANES = 128
HEAD_PAIRS = A_HEADS // 2
RET_PAIRS = B_HEADS // 2
VMEM_LIMIT = 48 * 1024 * 1024

REST_GA = 0
REST_QB = 4
REST_KB = 6
REST_VB = 8
REST_GB = 10
REST_UA = 6
REST_UB = 7
REST_GC = 8

CONTRACT_LAST = (((1,), (1,)), ((), ()))
CONTRACT_FIRST = (((0,), (0,)), ((), ()))


def _cparams(semantics):
    return pltpu.CompilerParams(dimension_semantics=semantics, vmem_limit_bytes=VMEM_LIMIT)


def _silu(x):
    return x * jax.nn.sigmoid(x)


def _layer_block(shape, layer, grid_rank):
    zeros = (0,) * len(shape)
    return pl.BlockSpec((None,) + tuple(shape), lambda *_: (layer,) + zeros)


INPROJ_ROWS = 512


K_TILE = 256
INPROJ_GROUP = 4


def _inproj_body(x_ref, g_ref, w_ref, rows=slice(None)):
    x = x_ref[rows, :]
    h = x * lax.rsqrt(jnp.mean(x * x, axis=-1, keepdims=True) + EPS) * g_ref[...]
    h = h.astype(BF16)

    def cols(c0, c1):
        return jnp.dot(h, w_ref[:, c0:c1], preferred_element_type=F32)

    def col_blocks(c0, n, width):
        accs = [None] * n
        for k0 in range(0, h.shape[1], K_TILE):
            hk = h[:, k0:k0 + K_TILE]
            for i in range(n):
                lo = c0 + i * width
                part = jnp.dot(hk, w_ref[k0:k0 + K_TILE, lo:lo + width], preferred_element_type=F32)
                accs[i] = part if accs[i] is None else accs[i] + part
        return accs

    cols.blocks = col_blocks
    return cols


def _inproj_prompt_kernel(x_ref, g_ref, w_ref, *refs):
    qkv_refs = (refs[0:3], refs[3:6], refs[6:9])
    kf_ref, vf_ref, rest_ref, stage_ref = refs[9:13]
    tm = x_ref.shape[0]
    rc = INPROJ_ROWS
    width = 2 * LANES
    n_blocks = IN_WIDTH // width

    def emit_qkv(rows, r0, t, hp0, val):
        if t == 0:
            val = val * (A_HEAD_DIM ** -0.5)
        else:
            (kf_ref, vf_ref)[t - 1][rows, hp0 * LANES:(hp0 + 2) * LANES] = val
        for hp in (hp0, hp0 + 1):
            piece = val[:, (hp - hp0) * LANES:(hp - hp0 + 1) * LANES]
            slab = t * HEAD_PAIRS + hp
            stage_ref[slab, rows, :] = piece
            for g, (_, r) in enumerate(DIL_GROUPS):
                out_ref = qkv_refs[t][g]
                if r == 1:
                    out_ref[hp, 0, rows, :] = piece.astype(BF16)
                else:
                    for c in range(r):
                        sub = stage_ref[slab, pl.ds(r0 + c, rc // r, stride=r), :]
                        out_ref[hp, c, r0 // r:(r0 + rc) // r, :] = sub.astype(BF16)

    for r0 in range(0, tm, rc):
        rows = slice(r0, r0 + rc)
        cols = _inproj_body(x_ref, g_ref, w_ref, rows)
        for b0 in range(0, n_blocks, INPROJ_GROUP):
            n = min(INPROJ_GROUP, n_blocks - b0)
            for i, val in enumerate(cols.blocks(b0 * width, n, width)):
                c0 = (b0 + i) * width
                if c0 < QKV_WIDTH:
                    emit_qkv(rows, r0, c0 // A_WIDTH, (c0 % A_WIDTH) // LANES, val)
                else:
                    rest_ref[rows, c0 - QKV_WIDTH:c0 - QKV_WIDTH + width] = val


def _inproj_prompt(x, g_all, w_all, layer, tm=512):
    B, S, D = x.shape
    nb = S // tm
    row = lambda width: pl.BlockSpec((None, tm, width), lambda b, i: (b, i, 0))
    grp_shapes, grp_specs = [], []
    for _ in range(3):
        for _, r in DIL_GROUPS:
            grp_shapes.append(jax.ShapeDtypeStruct((B, HEAD_PAIRS, r, S // r, LANES), BF16))
            grp_specs.append(pl.BlockSpec((None, HEAD_PAIRS, r, tm // r, LANES), lambda b, i: (b, 0, 0, i, 0)))
    outs = pl.pallas_call(
        _inproj_prompt_kernel,
        grid=(B, nb),
        in_specs=[row(D), _layer_block((1, D), layer, 2), _layer_block((D, IN_WIDTH), layer, 2)],
        out_specs=grp_specs + [row(A_WIDTH), row(A_WIDTH), row(REST_WIDTH)],
        out_shape=grp_shapes + [jax.ShapeDtypeStruct((B, S, A_WIDTH), F32),
                                jax.ShapeDtypeStruct((B, S, A_WIDTH), F32),
                                jax.ShapeDtypeStruct((B, S, REST_WIDTH), F32)],
        scratch_shapes=[pltpu.VMEM((3 * HEAD_PAIRS, tm, LANES), F32)],
        compiler_params=_cparams(("arbitrary", "arbitrary")),
        name="inproj_prompt",
    )(x, g_all, w_all)
    return outs[0:9], outs[9], outs[10], outs[11]


def _inproj_sample_kernel(x_ref, g_ref, w_ref, q_ref, kf_ref, vf_ref, rest_ref):
    cols = _inproj_body(x_ref, g_ref, w_ref)
    q_ref[...] = cols(0, A_WIDTH) * (A_HEAD_DIM ** -0.5)
    kf_ref[...] = cols(A_WIDTH, 2 * A_WIDTH)
    vf_ref[...] = cols(2 * A_WIDTH, 3 * A_WIDTH)
    for c0 in range(0, REST_WIDTH, 768):
        rest_ref[:, c0:c0 + 768] = cols(QKV_WIDTH + c0, QKV_WIDTH + c0 + 768)


def _inproj_sample(x, g_all, w_all, layer, tm=256):
    N, D = x.shape
    row = lambda width: pl.BlockSpec((tm, width), lambda i: (i, 0))
    return pl.pallas_call(
        _inproj_sample_kernel,
        grid=(N // tm,),
        in_specs=[row(D), _layer_block((1, D), layer, 1), _layer_block((D, IN_WIDTH), layer, 1)],
        out_specs=[row(A_WIDTH), row(A_WIDTH), row(A_WIDTH), row(REST_WIDTH)],
        out_shape=[jax.ShapeDtypeStruct((N, A_WIDTH), F32)] * 3
                  + [jax.ShapeDtypeStruct((N, REST_WIDTH), F32)],
        compiler_params=_cparams(("arbitrary",)),
        name="inproj_sample",
    )(x, g_all, w_all)


def _outproj_kernel(x_ref, ma_ref, mb_ref, mc_ref, w_ref, g_ref, o_ref, *, final_norm):
    y = x_ref[...]
    y = y + jnp.dot(ma_ref[...], w_ref[0:A_WIDTH, :], preferred_element_type=F32)
    y = y + jnp.dot(mb_ref[...], w_ref[A_WIDTH:A_WIDTH + B_WIDTH, :], preferred_element_type=F32)
    y = y + jnp.dot(mc_ref[...], w_ref[A_WIDTH + B_WIDTH:, :], preferred_element_type=F32)
    if final_norm:
        y = y * lax.rsqrt(jnp.mean(y * y, axis=-1, keepdims=True) + EPS) * g_ref[...]
    o_ref[...] = y


def _outproj(x, ma, mb, mc, w_all, layer, g_final, final_norm, tm):
    N, D = x.shape
    row = lambda width: pl.BlockSpec((tm, width), lambda i: (i, 0))
    return pl.pallas_call(
        functools.partial(_outproj_kernel, final_norm=final_norm),
        grid=(N // tm,),
        in_specs=[row(D), row(A_WIDTH), row(B_WIDTH), row(C_WIDTH),
                  _layer_block((D, D), layer, 1),
                  pl.BlockSpec((1, D), lambda i: (0, 0))],
        out_specs=row(D),
        out_shape=jax.ShapeDtypeStruct((N, D), F32),
        compiler_params=_cparams(("arbitrary",)),
        name="outproj",
    )(x, ma, mb, mc, w_all, g_final.reshape(1, D))


def _alibi_slopes():
    return np.exp2(-8.0 * (np.arange(A_HEADS, dtype=np.float64) + 1.0) / A_HEADS).astype(np.float32)


def _prompt_bias():
    slopes = _alibi_slopes()
    qi = np.arange(ATT_BLK)[:, None]
    ki = np.arange(2 * ATT_BLK)[None, :] - ATT_BLK
    dist = qi - ki
    tiles = []
    for w, r in DIL_GROUPS:
        n_back = w // r
        valid = (dist >= 0) & (dist <= n_back)
        pen = (slopes * np.float32(r))[:, None, None] * dist.astype(np.float32)[None]
        b = np.where(valid[None], -pen, np.float32(NEG)).astype(np.float32)
        tiles.append(b.reshape(HEAD_PAIRS, 2 * ATT_BLK, 2 * ATT_BLK))
    normal = np.stack(tiles)
    first = np.concatenate([normal[..., ATT_BLK:], np.full_like(normal[..., ATT_BLK:], NEG)], axis=-1)
    return jnp.asarray(np.stack([normal, first]))


def _attn_unit(q, kwin, vwin, bias, lo_half):
    zero = jnp.zeros_like(q)
    q2 = jnp.concatenate([jnp.where(lo_half, q, zero), jnp.where(lo_half, zero, q)], axis=0)
    s = lax.dot_general(q2, kwin, CONTRACT_LAST, preferred_element_type=F32) + bias
    m = jnp.max(s, axis=-1, keepdims=True)
    p = jnp.exp(s - m)
    den = jnp.sum(p, axis=-1, keepdims=True)
    o = jnp.dot(p.astype(BF16), vwin, preferred_element_type=F32) / den
    lse = m + jnp.log(den)
    o_pair = jnp.where(lo_half, o[:ATT_BLK], o[ATT_BLK:])
    lse_pair = jnp.where(lo_half, lse[:ATT_BLK], lse[ATT_BLK:])
    return o_pair, lse_pair


ATTN_UNROLL = 16


def _attn_prompt_group(g, q_ref, k_ref, v_ref, bias_ref, o_ref, l_ref):
    r = DIL_GROUPS[g][1]
    S = o_ref.shape[0]
    nblk = S // r // ATT_BLK
    lo_half = lax.broadcasted_iota(jnp.int32, (ATT_BLK, LANES), 1) < A_HEAD_DIM

    def trip(it, carry):
        for u in range(ATTN_UNROLL):
            n = it * ATTN_UNROLL + u
            blk, c = (n, 0) if r == 1 else (lax.div(n, jnp.int32(r)), lax.rem(n, jnp.int32(r)))
            q0 = pl.multiple_of(blk * ATT_BLK, ATT_BLK)
            q = q_ref[c, pl.ds(q0, ATT_BLK), :]
            if nblk == 1:
                kwin, vwin = k_ref[c], v_ref[c]
                bias = bias_ref[0, g, :, ATT_BLK:]
            else:
                k0 = pl.multiple_of(jnp.maximum(blk - 1, 0) * ATT_BLK, ATT_BLK)
                kwin = k_ref[c, pl.ds(k0, 2 * ATT_BLK), :]
                vwin = v_ref[c, pl.ds(k0, 2 * ATT_BLK), :]
                bias = bias_ref[jnp.where(blk == 0, 1, 0), g]
            o_pair, lse_pair = _attn_unit(q, kwin, vwin, bias, lo_half)
            rows = pl.ds(q0, ATT_BLK) if r == 1 else pl.ds(blk * (ATT_BLK * r) + c, ATT_BLK, stride=r)
            o_ref[rows, :] = o_pair
            l_ref[rows, :] = lse_pair
        return carry

    lax.fori_loop(0, (r * nblk) // ATTN_UNROLL, trip, 0)


def _attn_prompt_merge(o_refs, l_refs, ga_ref, out_ref):
    o1_ref, o4_ref, o16_ref = o_refs
    l1_ref, l4_ref, l16_ref = l_refs
    S = out_ref.shape[0]
    rows_per = 256

    def merge(i, carry):
        rows = pl.ds(pl.multiple_of(i * rows_per, rows_per), rows_per)
        l1, l4, l16 = l1_ref[rows, :], l4_ref[rows, :], l16_ref[rows, :]
        mx = jnp.maximum(jnp.maximum(l1, l4), l16)
        e1, e4, e16 = jnp.exp(l1 - mx), jnp.exp(l4 - mx), jnp.exp(l16 - mx)
        inv = 1.0 / (e1 + e4 + e16)
        y = (e1 * inv) * o1_ref[rows, :] + (e4 * inv) * o4_ref[rows, :] + (e16 * inv) * o16_ref[rows, :]
        out_ref[rows, :] = (y * _silu(ga_ref[rows, :])).astype(BF16)
        return carry

    lax.fori_loop(0, S // rows_per, merge, 0)


def _sample_bias(l_buf, T):
    slopes = _alibi_slopes()
    t = np.arange(T)[:, None]
    rho = np.arange(l_buf)[None, :]
    tn = np.arange(LANES)[None, :]
    neg = np.float32(NEG)
    cache, new = [], []
    for w, r in DIL_GROUPS:
        n_back = w // r
        sl = (slopes * np.float32(r))[:, None, None]
        dc = l_buf + t - rho
        jc = dc // r
        vc = (dc % r == 0) & (jc <= n_back)
        cache.append(np.where(vc[None], -(sl * jc.astype(np.float32)[None]), neg).reshape(A_HEADS * T, l_buf))
        dn = t - tn
        jn = dn // r
        vn = (dn >= 0) & (dn % r == 0) & (jn <= n_back)
        new.append(np.where(vn[None], -(sl * jn.astype(np.float32)[None]), neg).reshape(A_HEADS * T, LANES))
    return (jnp.asarray(np.stack(cache).astype(np.float32)), jnp.asarray(np.stack(new).astype(np.float32)))


def _attn_sample_kernel(q_ref, kn_ref, vn_ref, kc_ref, vc_ref, bc_ref, bn_ref, ga_ref, out_ref):
    T = q_ref.shape[0]
    l_buf = kc_ref.shape[-1]
    lo_half = lax.broadcasted_iota(jnp.int32, (T, LANES), 1) < A_HEAD_DIM
    pad = jnp.zeros((LANES - T, A_WIDTH), F32)
    kn = jnp.concatenate([kn_ref[...], pad], axis=0).astype(BF16)
    vn = jnp.concatenate([vn_ref[...], pad], axis=0).astype(BF16)
    q = q_ref[...]
    sc_parts, sn_parts = [], []
    for hp in range(HEAD_PAIRS):
        lanes = slice(hp * LANES, (hp + 1) * LANES)
        qp = q[:, lanes]
        q2 = jnp.concatenate([jnp.where(lo_half, qp, 0.0), jnp.where(lo_half, 0.0, qp)], axis=0).astype(BF16)
        kt = kc_ref[2 * hp:2 * hp + 2].reshape(LANES, l_buf).astype(BF16)
        sc_parts.append(jnp.dot(q2, kt, preferred_element_type=F32))
        sn_parts.append(lax.dot_general(q2, kn[:, lanes], CONTRACT_LAST, preferred_element_type=F32))
    sc = jnp.concatenate(sc_parts, axis=0)
    sn = jnp.concatenate(sn_parts, axis=0)
    starts = [l_buf - min(w, l_buf) for w, _ in DIL_GROUPS]
    pcs, pns, lses, dens = [], [], [], []
    for g, c0 in enumerate(starts):
        scg = sc[:, c0:] + bc_ref[g, :, c0:]
        sng = sn + bn_ref[g]
        m = jnp.maximum(jnp.max(scg, axis=-1, keepdims=True), jnp.max(sng, axis=-1, keepdims=True))
        pc = jnp.exp(scg - m)
        pn = jnp.exp(sng - m)
        den = jnp.sum(pc, axis=-1, keepdims=True) + jnp.sum(pn, axis=-1, keepdims=True)
        pcs.append(pc)
        pns.append(pn)
        dens.append(den)
        lses.append(m + jnp.log(den))
    mx = jnp.maximum(jnp.maximum(lses[0], lses[1]), lses[2])
    es = [jnp.exp(l - mx) for l in lses]
    tot = es[0] + es[1] + es[2]
    coef = [(e / tot) / d for e, d in zip(es, dens)]
    edges = sorted(set(starts)) + [l_buf]
    segs = []
    for lo, hi in zip(edges[:-1], edges[1:]):
        terms = [coef[g] * pcs[g][:, lo - c0:hi - c0] for g, c0 in enumerate(starts) if c0 <= lo]
        segs.append(functools.reduce(lambda a, b: a + b, terms))
    pc = jnp.concatenate(segs, axis=1).astype(BF16)
    pn = (coef[0] * pns[0] + coef[1] * pns[1] + coef[2] * pns[2]).astype(BF16)
    ys = []
    for hp in range(HEAD_PAIRS):
        lanes = slice(hp * LANES, (hp + 1) * LANES)
        rows = slice(2 * T * hp, 2 * T * (hp + 1))
        vt = vc_ref[2 * hp:2 * hp + 2].reshape(LANES, l_buf).astype(BF16)
        o = lax.dot_general(pc[rows], vt, CONTRACT_LAST, preferred_element_type=F32)
        o = o + jnp.dot(pn[rows], vn[:, lanes], preferred_element_type=F32)
        ys.append(jnp.where(lo_half, o[:T], o[T:]))
    y = jnp.concatenate(ys, axis=1)
    out_ref[...] = (y * _silu(ga_ref[...])).astype(BF16)


N_PHASES = len(DIL_GROUPS) + 1


CACHE_SLOTS = 3
ATTN_VMEM_LIMIT = 56 * 1024 * 1024


def _attn_fused_kernel(*refs, layer):
    n_groups = len(DIL_GROUPS)
    qkv = refs[0:3 * n_groups]
    bias_ref, ga_ref = refs[3 * n_groups:3 * n_groups + 2]
    q_ref, kn_ref, vn_ref, kc_hbm, vc_hbm, bc_ref, bn_ref, gs_ref = refs[3 * n_groups + 2:3 * n_groups + 10]
    out_p_ref, out_s_ref = refs[3 * n_groups + 10:3 * n_groups + 12]
    scratch = refs[3 * n_groups + 12:]
    o_refs, l_refs = scratch[0:n_groups], scratch[n_groups:2 * n_groups]
    kbuf, vbuf, sem = scratch[2 * n_groups:2 * n_groups + 3]
    j = pl.program_id(2)
    n_steps = pl.num_programs(0) * pl.num_programs(1) * pl.num_programs(2)
    n = (pl.program_id(0) * pl.num_programs(1) + pl.program_id(1)) * pl.num_programs(2) + j

    def cache_copies(seq, slot):
        return (pltpu.make_async_copy(kc_hbm.at[layer, seq], kbuf.at[slot], sem.at[0, slot]),
                pltpu.make_async_copy(vc_hbm.at[layer, seq], vbuf.at[slot], sem.at[1, slot]))

    def start(seq):
        for c in cache_copies(seq, lax.rem(seq, CACHE_SLOTS)):
            c.start()

    @pl.when(n == 0)
    def _():
        for seq in range(CACHE_SLOTS - 1):
            start(jnp.int32(seq))

    @pl.when(n + (CACHE_SLOTS - 1) < n_steps)
    def _():
        start(n + (CACHE_SLOTS - 1))

    slot = lax.rem(n, CACHE_SLOTS)
    for c in cache_copies(n, slot):
        c.wait()
    sample_in = (q_ref, kn_ref, vn_ref, kbuf.at[slot], vbuf.at[slot], bc_ref, bn_ref, gs_ref)

    for g in range(n_groups):
        @pl.when(j == g)
        def _(g=g):
            _attn_prompt_group(g, qkv[g], qkv[n_groups + g], qkv[2 * n_groups + g], bias_ref, o_refs[g], l_refs[g])

    @pl.when(j == n_groups)
    def _():
        _attn_prompt_merge(o_refs, l_refs, ga_ref, out_p_ref)

    _attn_sample_kernel(*sample_in, out_s_ref)


def _attn_fused(qkv_groups, rest_p, bias_p, q, kn, vn, cache_kt, cache_vt, layer, bias_c, bias_n, rest_s):
    B, S, _ = rest_p.shape
    Bd, T, _ = q.shape
    l_buf = cache_kt.shape[-1]
    HT = A_HEADS * T
    G = len(DIL_GROUPS)
    J = Bd // (B * HEAD_PAIRS)
    assert Bd == B * HEAD_PAIRS * J and J >= N_PHASES and (S // ATT_BLK) % ATTN_UNROLL == 0
    seq = lambda b, hp, j: (b * HEAD_PAIRS + hp) * J + j
    ins = list(qkv_groups)
    specs = [pl.BlockSpec((None, None) + t.shape[2:], lambda b, hp, j: (b, hp, 0, 0, 0)) for t in ins]
    tok = pl.BlockSpec((None, T, A_WIDTH), lambda b, hp, j: (seq(b, hp, j), 0, 0))
    cache = pl.BlockSpec(memory_space=pl.ANY)
    cache_slots = pltpu.VMEM((CACHE_SLOTS, A_HEADS, A_HEAD_DIM, l_buf), cache_kt.dtype)
    ins += [bias_p, rest_p, q, kn, vn, cache_kt, cache_vt, bias_c, bias_n, rest_s]
    specs += [pl.BlockSpec((2, G, None, 2 * ATT_BLK, 2 * ATT_BLK), lambda b, hp, j: (0, 0, hp, 0, 0)),
              pl.BlockSpec((None, S, LANES), lambda b, hp, j: (b, 0, REST_GA + hp)),
              tok, tok, tok, cache, cache,
              pl.BlockSpec((G, HT, l_buf), lambda b, hp, j: (0, 0, 0)),
              pl.BlockSpec((G, HT, LANES), lambda b, hp, j: (0, 0, 0)),
              pl.BlockSpec((None, T, A_WIDTH), lambda b, hp, j: (seq(b, hp, j), 0, REST_GA))]
    return pl.pallas_call(
        functools.partial(_attn_fused_kernel, layer=layer),
        grid=(B, HEAD_PAIRS, J),
        in_specs=specs,
        out_specs=[pl.BlockSpec((None, S, LANES), lambda b, hp, j: (b, 0, hp)), tok],
        out_shape=[jax.ShapeDtypeStruct((B, S, A_WIDTH), BF16),
                   jax.ShapeDtypeStruct((Bd, T, A_WIDTH), BF16)],
        scratch_shapes=[pltpu.VMEM((S, LANES), F32)] * (2 * G)
                       + [cache_slots, cache_slots, pltpu.SemaphoreType.DMA((2, CACHE_SLOTS))],
        compiler_params=pltpu.CompilerParams(dimension_semantics=("arbitrary", "arbitrary", "arbitrary"),
                                             vmem_limit_bytes=ATTN_VMEM_LIMIT),
        name="attn_fused",
    )(*ins)


def _retention_consts(chunk, nseq):
    H = B_HEADS
    log_g = np.log(1.0 - np.exp2(-5.0 - np.arange(H, dtype=np.float64)))
    i = np.arange(chunk, dtype=np.float64)
    diff = i[:, None] - i[None, :]
    dmat = np.where(diff >= 0, np.exp(np.maximum(diff, 0.0)[None] * log_g[:, None, None]), 0.0)
    q_dec = np.exp((i[:, None] + 1.0) * log_g[None, :])
    k_dec = np.exp((chunk - 1.0 - i)[:, None] * log_g[None, :])
    c_dec = np.exp(chunk * log_g)
    rows = nseq * chunk
    eye = np.eye(nseq)
    dmat = (eye[None, :, None, :, None] * dmat[:, None, :, None, :]).reshape(H, rows, rows)
    lanes = lambda t: np.repeat(t, B_DIM, axis=-1)
    pair = lambda t: t.reshape(t.shape[0], RET_PAIRS, LANES).transpose(1, 0, 2)
    dmat2 = dmat.reshape(RET_PAIRS, 2 * rows, rows)
    qd = pair(lanes(np.tile(q_dec, (nseq, 1))))
    kd = pair(lanes(np.tile(k_dec, (nseq, 1))))
    cd = lanes(c_dec[None, :]).reshape(RET_PAIRS, 1, LANES)
    cd = np.broadcast_to(cd, (RET_PAIRS, LANES, LANES))
    return tuple(jnp.asarray(np.ascontiguousarray(a).astype(np.float32)) for a in (dmat2, qd, kd, cd))


def _ret_intra(qf, kf, v, dmat, lo_half):
    rows = qf.shape[0]
    q2 = jnp.concatenate([jnp.where(lo_half, qf, 0.0), jnp.where(lo_half, 0.0, qf)], axis=0).astype(BF16)
    att = lax.dot_general(q2, kf.astype(BF16), CONTRACT_LAST, preferred_element_type=F32) * dmat
    oi = jnp.dot(att.astype(BF16), v, preferred_element_type=F32)
    return jnp.where(lo_half, oi[:rows], oi[rows:])


def _ret_norm_gate(o, gn, gate, lo_half):
    def head_sum(x):
        a = jnp.sum(jnp.where(lo_half, x, 0.0), axis=-1, keepdims=True)
        b = jnp.sum(jnp.where(lo_half, 0.0, x), axis=-1, keepdims=True)
        return jnp.where(lo_half, a, b)

    mu = head_sum(o) / B_DIM
    d = o - mu
    var = head_sum(d * d) / B_DIM
    y = d * lax.rsqrt(var + EPS) * gn
    return (y * _silu(gate)).astype(BF16)


def _same_head_mask():
    r2 = lax.broadcasted_iota(jnp.int32, (LANES, LANES), 0) < B_DIM
    c2 = lax.broadcasted_iota(jnp.int32, (LANES, LANES), 1) < B_DIM
    return r2 == c2


RET_UNROLL = 4


def _retention_prompt_kernel(q_ref, k_ref, v_ref, gate_ref, s0_ref, dmat_ref, qd_ref, kd_ref, cd_ref, gn_ref,
                             out_ref, sfin_ref, oi_ref, kv_ref, st_ref, *, chunk):
    L = q_ref.shape[0]
    nc = L // chunk
    lo_half = lax.broadcasted_iota(jnp.int32, (chunk, LANES), 1) < B_DIM
    same_head = _same_head_mask()

    def chunk_rows(ci):
        return pl.ds(pl.multiple_of(ci * chunk, chunk), chunk)

    def phase1(it, carry):
        for u in range(RET_UNROLL):
            ci = it * RET_UNROLL + u
            rows = chunk_rows(ci)
            qf = q_ref[rows, :]
            kf = k_ref[rows, :] * (B_DIM ** -0.5)
            v = v_ref[rows, :].astype(BF16)
            oi_ref[rows, :] = _ret_intra(qf, kf, v, dmat_ref[...], lo_half)
            kv = lax.dot_general((kf * kd_ref[...]).astype(BF16), v, CONTRACT_FIRST, preferred_element_type=F32)
            kv_ref[ci] = jnp.where(same_head, kv, 0.0)
        return carry

    lax.fori_loop(0, nc // RET_UNROLL, phase1, 0)

    def phase2(ci, state):
        st_ref[ci] = state.astype(BF16)
        return state * cd_ref[...] + kv_ref[ci]

    sfin_ref[...] = lax.fori_loop(0, nc, phase2, s0_ref[...])

    def phase3(it, carry):
        for u in range(RET_UNROLL):
            ci = it * RET_UNROLL + u
            rows = chunk_rows(ci)
            inter = jnp.dot(q_ref[rows, :].astype(BF16), st_ref[ci], preferred_element_type=F32) * qd_ref[...]
            out_ref[rows, :] = _ret_norm_gate(oi_ref[rows, :] + inter, gn_ref[...], gate_ref[rows, :], lo_half)
        return carry

    lax.fori_loop(0, nc // RET_UNROLL, phase3, 0)


def _retention_prompt(rest, s0_bd, consts, gn_all, layer, chunk):
    B, L, _ = rest.shape
    assert (L // chunk) % RET_UNROLL == 0
    dmat2, qd, kd, cd = consts
    col = lambda blk: pl.BlockSpec((None, L, LANES), lambda b, p, blk=blk: (b, 0, blk + p))
    pair_const = lambda a: pl.BlockSpec((None,) + a.shape[1:], lambda b, p: (p,) + (0,) * (a.ndim - 1))
    st = pl.BlockSpec((None, None, LANES, LANES), lambda b, p: (b, p, 0, 0))
    gn_spec = pl.BlockSpec((None, None, 1, LANES), lambda b, p: (layer, p, 0, 0))
    return pl.pallas_call(
        functools.partial(_retention_prompt_kernel, chunk=chunk),
        grid=(B, RET_PAIRS),
        in_specs=[col(REST_QB), col(REST_KB), col(REST_VB), col(REST_GB), st,
                  pair_const(dmat2), pair_const(qd), pair_const(kd), pair_const(cd), gn_spec],
        out_specs=[pl.BlockSpec((None, L, LANES), lambda b, p: (b, 0, p)), st],
        out_shape=[jax.ShapeDtypeStruct((B, L, B_WIDTH), BF16),
                   jax.ShapeDtypeStruct((B, RET_PAIRS, LANES, LANES), F32)],
        scratch_shapes=[pltpu.VMEM((L, LANES), F32),
                        pltpu.VMEM((L // chunk, LANES, LANES), F32),
                        pltpu.VMEM((L // chunk, LANES, LANES), BF16)],
        compiler_params=_cparams(("arbitrary", "arbitrary")),
        name="retention_prompt",
    )(rest, rest, rest, rest, s0_bd, dmat2, qd, kd, cd, gn_all)


def _retention_sample_kernel(q_ref, k_ref, v_ref, gate_ref, s0_ref, dmat_ref, qd_ref, kd_ref, cd_ref, gn_ref,
                             out_ref, sfin_ref, *, T):
    rows = q_ref.shape[0]
    nseq = rows // T
    lo_half = lax.broadcasted_iota(jnp.int32, (rows, LANES), 1) < B_DIM
    seq_of_row = lax.broadcasted_iota(jnp.int32, (rows, LANES), 0) // T

    def expand(x):
        return jnp.concatenate([jnp.where(seq_of_row == s, x, 0.0) for s in range(nseq)], axis=1).astype(BF16)

    qf = q_ref[...]
    kf = k_ref[...] * (B_DIM ** -0.5)
    v = v_ref[...].astype(BF16)
    s0 = s0_ref[...]
    o = _ret_intra(qf, kf, v, dmat_ref[...], lo_half)
    s0_rows = s0.reshape(nseq * LANES, LANES).astype(BF16)
    o = o + jnp.dot(expand(qf), s0_rows, preferred_element_type=F32) * qd_ref[...]
    kv = lax.dot_general(expand(kf * kd_ref[...]), v, CONTRACT_FIRST, preferred_element_type=F32)
    out_ref[...] = _ret_norm_gate(o, gn_ref[...], gate_ref[...], lo_half)
    sfin_ref[...] = s0 * cd_ref[...][None] + jnp.where(_same_head_mask()[None],
                                                      kv.reshape(nseq, LANES, LANES), 0.0)


def _retention_sample(rest, s0_bd, consts, gn_all, layer, T, nseq=16):
    N, _ = rest.shape
    rows = nseq * T
    dmat2, qd, kd, cd = consts
    col = lambda blk: pl.BlockSpec((rows, LANES), lambda i, p, blk=blk: (i, blk + p))
    pair_const = lambda a: pl.BlockSpec((None,) + a.shape[1:], lambda i, p: (p,) + (0,) * (a.ndim - 1))
    st = pl.BlockSpec((nseq, None, LANES, LANES), lambda i, p: (i, p, 0, 0))
    gn_spec = pl.BlockSpec((None, None, 1, LANES), lambda i, p: (layer, p, 0, 0))
    return pl.pallas_call(
        functools.partial(_retention_sample_kernel, T=T),
        grid=(N // rows, RET_PAIRS),
        in_specs=[col(REST_QB), col(REST_KB), col(REST_VB), col(REST_GB), st,
                  pair_const(dmat2), pair_const(qd), pair_const(kd), pair_const(cd), gn_spec],
        out_specs=[pl.BlockSpec((rows, LANES), lambda i, p: (i, p)), st],
        out_shape=[jax.ShapeDtypeStruct((N, B_WIDTH), BF16),
                   jax.ShapeDtypeStruct(s0_bd.shape, F32)],
        compiler_params=_cparams(("arbitrary", "arbitrary")),
        name="retention_sample",
    )(rest, rest, rest, rest, s0_bd, dmat2, qd, kd, cd, gn_all)


def _state_to_pairs(s):
    B = s.shape[0]
    s = s.reshape(B, RET_PAIRS, 2, B_DIM, B_DIM)
    z = jnp.zeros_like(s[:, :, 0])
    top = jnp.concatenate([s[:, :, 0], z], axis=-1)
    bot = jnp.concatenate([z, s[:, :, 1]], axis=-1)
    return jnp.concatenate([top, bot], axis=-2)


def _pairs_to_state(sbd):
    B = sbd.shape[0]
    a = sbd[:, :, :B_DIM, :B_DIM]
    b = sbd[:, :, B_DIM:, B_DIM:]
    return jnp.stack([a, b], axis=2).reshape(B, B_HEADS, B_DIM, B_DIM)


CONV_STATE = CONV_W - 1
CONV_PAD = 32
LANE_HALVES = C_WIDTH // LANES


def _conv_param_specs(layer, grid_rank):
    vec = _layer_block((1, C_WIDTH), layer, grid_rank)
    return [_layer_block((CONV_W, C_WIDTH), layer, grid_rank), vec, vec, vec,
            _layer_block((C_WIDTH, C_WIDTH), layer, grid_rank), vec]


def _conv_kernel(ua_ref, ub_ref, gc_ref, s0_ref, dww_ref, dwb_ref, lng_ref, lnb_ref, pww_ref, pwb_ref,
                 out_ref, sfin_ref, ext_ref, *, tile):
    L = ua_ref.shape[0]
    u = ua_ref[...] * jax.nn.sigmoid(ub_ref[...])
    s0 = s0_ref[...]
    for h in range(LANE_HALVES):
        lanes = slice(h * LANES, (h + 1) * LANES)
        ext_ref[h, 0:CONV_PAD, :] = s0[:, lanes]
        ext_ref[h, CONV_PAD:CONV_PAD + L, :] = u[:, lanes]
    for h in range(LANE_HALVES):
        sfin_ref[:, h * LANES:(h + 1) * LANES] = ext_ref[h, L:L + CONV_PAD, :]

    def step(i, carry):
        t0 = pl.multiple_of(i * tile, tile)
        halves = []
        for h in range(LANE_HALVES):
            acc = jnp.zeros((tile, LANES), F32)
            for w in range(CONV_W):
                rows = pl.ds(t0 + (CONV_PAD - CONV_STATE + w), tile, stride=1)
                acc = acc + ext_ref[h, rows, :] * dww_ref[w:w + 1, h * LANES:(h + 1) * LANES]
            halves.append(acc)
        y = jnp.concatenate(halves, axis=1) + dwb_ref[...]
        mu = jnp.mean(y, axis=-1, keepdims=True)
        d = y - mu
        var = jnp.mean(d * d, axis=-1, keepdims=True)
        yn = d * lax.rsqrt(var + EPS) * lng_ref[...] + lnb_ref[...]
        yn = _silu(yn).astype(BF16)
        z = jnp.dot(yn, pww_ref[...], preferred_element_type=F32) + pwb_ref[...]
        rows = pl.ds(t0, tile)
        out_ref[rows, :] = (z * _silu(gc_ref[rows, :])).astype(BF16)
        return carry

    lax.fori_loop(0, L // tile, step, 0)


def _conv(rest, s0, conv_params, layer, tile):
    B, L, _ = rest.shape
    col = lambda blk: pl.BlockSpec((None, L, C_WIDTH), lambda b, blk=blk: (b, 0, blk))
    st = pl.BlockSpec((None, CONV_PAD, C_WIDTH), lambda b: (b, 0, 0))
    s0 = jnp.pad(s0, ((0, 0), (CONV_PAD - CONV_STATE, 0), (0, 0)))
    out, sfin = pl.pallas_call(
        functools.partial(_conv_kernel, tile=tile),
        grid=(B,),
        in_specs=[col(REST_UA), col(REST_UB), col(REST_GC), st] + _conv_param_specs(layer, 1),
        out_specs=[pl.BlockSpec((None, L, C_WIDTH), lambda b: (b, 0, 0)), st],
        out_shape=[jax.ShapeDtypeStruct((B, L, C_WIDTH), BF16),
                   jax.ShapeDtypeStruct((B, CONV_PAD, C_WIDTH), F32)],
        scratch_shapes=[pltpu.VMEM((LANE_HALVES, CONV_PAD + L, LANES), F32)],
        compiler_params=_cparams(("arbitrary",)),
        name="conformer_conv",
    )(rest, rest, rest, s0, *conv_params)
    return out, sfin[:, CONV_PAD - CONV_STATE:]


def _conv_sample_kernel(ua_ref, ub_ref, gc_ref, s0_ref, dww_ref, dwb_ref, lng_ref, lnb_ref, pww_ref, pwb_ref,
                        out_ref, sfin_ref, u_ref, ut_ref, z_ref, *, T):
    N = ua_ref.shape[0]
    Bd = N // T
    u = ua_ref[...] * jax.nn.sigmoid(ub_ref[...])
    for h in range(LANE_HALVES):
        u_ref[h] = u[:, h * LANES:(h + 1) * LANES]
    for t in range(T):
        ut_ref[t] = jnp.concatenate([u_ref[h, pl.ds(t, Bd, stride=T), :] for h in range(LANE_HALVES)], axis=1)

    def ext(tau):
        return s0_ref[tau] if tau < CONV_STATE else ut_ref[tau - CONV_STATE]

    for tau in range(CONV_STATE):
        sfin_ref[tau] = ext(tau + T)
    for t in range(T):
        acc = jnp.zeros((Bd, C_WIDTH), F32)
        for w in range(CONV_W):
            acc = acc + ext(t + w) * dww_ref[w:w + 1, :]
        y = acc + dwb_ref[...]
        mu = jnp.mean(y, axis=-1, keepdims=True)
        d = y - mu
        var = jnp.mean(d * d, axis=-1, keepdims=True)
        yn = d * lax.rsqrt(var + EPS) * lng_ref[...] + lnb_ref[...]
        yn = _silu(yn).astype(BF16)
        z = jnp.dot(yn, pww_ref[...], preferred_element_type=F32) + pwb_ref[...]
        for h in range(LANE_HALVES):
            z_ref[h, pl.ds(t, Bd, stride=T), :] = z[:, h * LANES:(h + 1) * LANES]
    z = jnp.concatenate([z_ref[h] for h in range(LANE_HALVES)], axis=1)
    out_ref[...] = (z * _silu(gc_ref[...])).astype(BF16)


def _conv_sample(rest, state_t, conv_params, layer, T):
    N, _ = rest.shape
    Bd = N // T
    col = lambda blk: pl.BlockSpec((N, C_WIDTH), lambda i, blk=blk: (0, blk))
    return pl.pallas_call(
        functools.partial(_conv_sample_kernel, T=T),
        grid=(1,),
        in_specs=[col(REST_UA), col(REST_UB), col(REST_GC),
                  pl.BlockSpec((None, CONV_STATE, Bd, C_WIDTH), lambda i: (layer, 0, 0, 0))]
                 + _conv_param_specs(layer, 1),
        out_specs=[pl.BlockSpec((N, C_WIDTH), lambda i: (0, 0)),
                   pl.BlockSpec((CONV_STATE, Bd, C_WIDTH), lambda i: (0, 0, 0))],
        out_shape=[jax.ShapeDtypeStruct((N, C_WIDTH), BF16),
                   jax.ShapeDtypeStruct((CONV_STATE, Bd, C_WIDTH), F32)],
        scratch_shapes=[pltpu.VMEM((LANE_HALVES, N, LANES), F32),
                        pltpu.VMEM((T, Bd, C_WIDTH), F32),
                        pltpu.VMEM((LANE_HALVES, N, LANES), F32)],
        compiler_params=_cparams(("arbitrary",)),
        name="conformer_conv_sample",
    )(rest, rest, rest, state_t, *conv_params)


def kernel(x_prompt, x_sample, cache_attn_k, cache_attn_v, state_retention, state_conv, norm_g, w_in, w_out,
           ret_norm_g, conv_dw_w, conv_dw_b, conv_ln_g, conv_ln_b, conv_pw_w, conv_pw_b, final_norm_g):
    Bp, S, D = x_prompt.shape
    Bd, T, _ = x_sample.shape
    depth = w_in.shape[0]
    l_buf = cache_attn_k.shape[2]
    assert S % (ATT_BLK * DIL_GROUPS[-1][1]) == 0 and S % RET_CHUNK == 0 and l_buf == WIN_MAX
    lbuf_p = min(WIN_MAX, S)
    ret_nseq = 128 // T
    ck = jnp.transpose(cache_attn_k, (0, 1, 3, 4, 2))
    cv = jnp.transpose(cache_attn_v, (0, 1, 3, 4, 2))
    conv_state_t = jnp.transpose(state_conv, (0, 2, 1, 3))

    w_in_b = w_in.astype(BF16)
    w_out_b = w_out.astype(BF16)
    norm_g3 = norm_g.reshape(depth, 1, D)
    ret_gn = ret_norm_g.reshape(depth, RET_PAIRS, 1, LANES)
    vec3 = lambda a: a.reshape(depth, 1, C_WIDTH)
    conv_params = (conv_dw_w, vec3(conv_dw_b), vec3(conv_ln_g), vec3(conv_ln_b), conv_pw_w.astype(BF16),
                   vec3(conv_pw_b))
    bias_p = _prompt_bias()
    bias_c, bias_n = _sample_bias(l_buf, T)
    ret_consts_p = _retention_consts(RET_CHUNK, 1)
    ret_consts_s = _retention_consts(T, ret_nseq)
    zero_ret = jnp.zeros((Bp, RET_PAIRS, LANES, LANES), F32)
    zero_conv = jnp.zeros((Bp, CONV_STATE, C_WIDTH), F32)

    xp = x_prompt
    xs = x_sample.reshape(Bd * T, D)
    kp_l, vp_l, ks_l, vs_l, rp_l, rs_l, cp_l, cs_l = [], [], [], [], [], [], [], []
    for l in range(depth):
        last = l == depth - 1
        qkv_groups, kf, vf, rest = _inproj_prompt(xp, norm_g3, w_in_b, l)
        q_s, kf_s, vf_s, rest_s = _inproj_sample(xs, norm_g3, w_in_b, l)
        tok = lambda a: a.reshape(Bd, T, -1)
        ma, ma_s = _attn_fused(qkv_groups, rest, bias_p, tok(q_s), tok(kf_s), tok(vf_s), ck, cv, l,
                               bias_c, bias_n, tok(rest_s))
        mb, rp = _retention_prompt(rest, zero_ret, ret_consts_p, ret_gn, l, RET_CHUNK)
        mc, cp = _conv(rest, zero_conv, conv_params, l, tile=512)
        xp = _outproj(xp.reshape(Bp * S, D), ma.reshape(Bp * S, -1), mb.reshape(Bp * S, -1),
                      mc.reshape(Bp * S, -1), w_out_b, l, final_norm_g, last, tm=512).reshape(Bp, S, D)
        kp_l.append(kf[:, S - lbuf_p:].reshape(Bp, lbuf_p, A_HEADS, A_HEAD_DIM))
        vp_l.append(vf[:, S - lbuf_p:].reshape(Bp, lbuf_p, A_HEADS, A_HEAD_DIM))
        rp_l.append(_pairs_to_state(rp))
        cp_l.append(cp)
        mb, rs = _retention_sample(rest_s, _state_to_pairs(state_retention[l]), ret_consts_s, ret_gn, l,
                                   T, ret_nseq)
        mc, cs = _conv_sample(rest_s, conv_state_t, conv_params, l, T=T)
        xs = _outproj(xs, ma_s.reshape(Bd * T, -1), mb, mc, w_out_b, l, final_norm_g, last, tm=256)
        ks_l.append(kf_s.reshape(Bd, T, A_HEADS, A_HEAD_DIM))
        vs_l.append(vf_s.reshape(Bd, T, A_HEADS, A_HEAD_DIM))
        rs_l.append(_pairs_to_state(rs))
        cs_l.append(cs)
    return (xp, xs.reshape(Bd, T, D),
            jnp.stack(kp_l), jnp.stack(vp_l), jnp.stack(ks_l), jnp.stack(vs_l),
            jnp.stack(rp_l), jnp.stack(rs_l), jnp.stack(cp_l), jnp.transpose(jnp.stack(cs_l), (0, 2, 1, 3)))
```

```python
import functools

import jax
import jax.numpy as jnp
import numpy as np
from jax import lax
from jax.experimental import pallas as pl
from jax.experimental.pallas import tpu as pltpu

F32 = jnp.float32
BF16 = jnp.bfloat16

D_MODEL = 1024
A_HEADS = 8
A_HEAD_DIM = 64
A_WIDTH = A_HEADS * A_HEAD_DIM
DIL_GROUPS = ((128, 1), (512, 4), (2048, 16))
WIN_MAX = 2048
ATT_BLK = 128
B_HEADS = 4
B_DIM = 64
B_WIDTH = B_HEADS * B_DIM
RET_CHUNK = 128
C_WIDTH = 256
CONV_W = 31
IN_WIDTH = 4 * A_WIDTH + 4 * B_WIDTH + 3 * C_WIDTH
QKV_WIDTH = 3 * A_WIDTH
REST_WIDTH = IN_WIDTH - QKV_WIDTH
EPS = 1e-6
NEG = -1e30
LANES = 128
HEAD_PAIRS = A_HEADS // 2
RET_PAIRS = B_HEADS // 2
VMEM_LIMIT = 48 * 1024 * 1024

REST_GA = 0
REST_QB = 4
REST_KB = 6
REST_VB = 8
REST_GB = 10
REST_UA = 6
REST_UB = 7
REST_GC = 8

CONTRACT_LAST = (((1,), (1,)), ((), ()))
CONTRACT_FIRST = (((0,), (0,)), ((), ()))


def _cparams(semantics):
    return pltpu.CompilerParams(dimension_semantics=semantics, vmem_limit_bytes=VMEM_LIMIT)


def _silu(x):
    return x * jax.nn.sigmoid(x)


def _layer_block(shape, layer, grid_rank):
    zeros = (0,) * len(shape)
    return pl.BlockSpec((None,) + tuple(shape), lambda *_: (layer,) + zeros)


INPROJ_ROWS = 512


K_TILE = 256
INPROJ_GROUP = 4


def _inproj_body(x_ref, g_ref, w_ref, rows=slice(None)):
    x = x_ref[rows, :]
    h = x * lax.rsqrt(jnp.mean(x * x, axis=-1, keepdims=True) + EPS) * g_ref[...]
    h = h.astype(BF16)

    def cols(c0, c1):
        return jnp.dot(h, w_ref[:, c0:c1], preferred_element_type=F32)

    def col_blocks(c0, n, width):
        accs = [None] * n
        for k0 in range(0, h.shape[1], K_TILE):
            hk = h[:, k0:k0 + K_TILE]
            for i in range(n):
                lo = c0 + i * width
                part = jnp.dot(hk, w_ref[k0:k0 + K_TILE, lo:lo + width], preferred_element_type=F32)
                accs[i] = part if accs[i] is None else accs[i] + part
        return accs

    cols.blocks = col_blocks
    return cols


def _inproj_prompt_kernel(x_ref, g_ref, w_ref, *refs):
    qkv_refs = (refs[0:3], refs[3:6], refs[6:9])
    kf_ref, vf_ref, rest_ref, stage_ref = refs[9:13]
    tm = x_ref.shape[0]
    rc = INPROJ_ROWS
    width = 2 * LANES
    n_blocks = IN_WIDTH // width

    def emit_qkv(rows, r0, t, hp0, val):
        if t == 0:
            val = val * (A_HEAD_DIM ** -0.5)
        else:
            (kf_ref, vf_ref)[t - 1][rows, hp0 * LANES:(hp0 + 2) * LANES] = val
        for hp in (hp0, hp0 + 1):
            piece = val[:, (hp - hp0) * LANES:(hp - hp0 + 1) * LANES]
            slab = t * HEAD_PAIRS + hp
            stage_ref[slab, rows, :] = piece
            for g, (_, r) in enumerate(DIL_GROUPS):
                out_ref = qkv_refs[t][g]
                if r == 1:
                    out_ref[hp, 0, rows, :] = piece.astype(BF16)
                else:
                    for c in range(r):
                        sub = stage_ref[slab, pl.ds(r0 + c, rc // r, stride=r), :]
                        out_ref[hp, c, r0 // r:(r0 + rc) // r, :] = sub.astype(BF16)

    for r0 in range(0, tm, rc):
        rows = slice(r0, r0 + rc)
        cols = _inproj_body(x_ref, g_ref, w_ref, rows)
        for b0 in range(0, n_blocks, INPROJ_GROUP):
            n = min(INPROJ_GROUP, n_blocks - b0)
            for i, val in enumerate(cols.blocks(b0 * width, n, width)):
                c0 = (b0 + i) * width
                if c0 < QKV_WIDTH:
                    emit_qkv(rows, r0, c0 // A_WIDTH, (c0 % A_WIDTH) // LANES, val)
                else:
                    rest_ref[rows, c0 - QKV_WIDTH:c0 - QKV_WIDTH + width] = val


def _inproj_prompt(x, g_all, w_all, layer, tm=512):
    B, S, D = x.shape
    nb = S // tm
    row = lambda width: pl.BlockSpec((None, tm, width), lambda b, i: (b, i, 0))
    grp_shapes, grp_specs = [], []
    for _ in range(3):
        for _, r in DIL_GROUPS:
            grp_shapes.append(jax.ShapeDtypeStruct((B, HEAD_PAIRS, r, S // r, LANES), BF16))
            grp_specs.append(pl.BlockSpec((None, HEAD_PAIRS, r, tm // r, LANES), lambda b, i: (b, 0, 0, i, 0)))
    outs = pl.pallas_call(
        _inproj_prompt_kernel,
        grid=(B, nb),
        in_specs=[row(D), _layer_block((1, D), layer, 2), _layer_block((D, IN_WIDTH), layer, 2)],
        out_specs=grp_specs + [row(A_WIDTH), row(A_WIDTH), row(REST_WIDTH)],
        out_shape=grp_shapes + [jax.ShapeDtypeStruct((B, S, A_WIDTH), F32),
                                jax.ShapeDtypeStruct((B, S, A_WIDTH), F32),
                                jax.ShapeDtypeStruct((B, S, REST_WIDTH), F32)],
        scratch_shapes=[pltpu.VMEM((3 * HEAD_PAIRS, tm, LANES), F32)],
        compiler_params=_cparams(("arbitrary", "arbitrary")),
        name="inproj_prompt",
    )(x, g_all, w_all)
    return outs[0:9], outs[9], outs[10], outs[11]


def _inproj_sample_kernel(x_ref, g_ref, w_ref, q_ref, kf_ref, vf_ref, rest_ref):
    cols = _inproj_body(x_ref, g_ref, w_ref)
    q_ref[...] = cols(0, A_WIDTH) * (A_HEAD_DIM ** -0.5)
    kf_ref[...] = cols(A_WIDTH, 2 * A_WIDTH)
    vf_ref[...] = cols(2 * A_WIDTH, 3 * A_WIDTH)
    for c0 in range(0, REST_WIDTH, 768):
        rest_ref[:, c0:c0 + 768] = cols(QKV_WIDTH + c0, QKV_WIDTH + c0 + 768)


def _inproj_sample(x, g_all, w_all, layer, tm=256):
    N, D = x.shape
    row = lambda width: pl.BlockSpec((tm, width), lambda i: (i, 0))
    return pl.pallas_call(
        _inproj_sample_kernel,
        grid=(N // tm,),
        in_specs=[row(D), _layer_block((1, D), layer, 1), _layer_block((D, IN_WIDTH), layer, 1)],
        out_specs=[row(A_WIDTH), row(A_WIDTH), row(A_WIDTH), row(REST_WIDTH)],
        out_shape=[jax.ShapeDtypeStruct((N, A_WIDTH), F32)] * 3
                  + [jax.ShapeDtypeStruct((N, REST_WIDTH), F32)],
        compiler_params=_cparams(("arbitrary",)),
        name="inproj_sample",
    )(x, g_all, w_all)


def _outproj_kernel(x_ref, ma_ref, mb_ref, mc_ref, w_ref, g_ref, o_ref, *, final_norm):
    y = x_ref[...]
    y = y + jnp.dot(ma_ref[...], w_ref[0:A_WIDTH, :], preferred_element_type=F32)
    y = y + jnp.dot(mb_ref[...], w_ref[A_WIDTH:A_WIDTH + B_WIDTH, :], preferred_element_type=F32)
    y = y + jnp.dot(mc_ref[...], w_ref[A_WIDTH + B_WIDTH:, :], preferred_element_type=F32)
    if final_norm:
        y = y * lax.rsqrt(jnp.mean(y * y, axis=-1, keepdims=True) + EPS) * g_ref[...]
    o_ref[...] = y


def _outproj(x, ma, mb, mc, w_all, layer, g_final, final_norm, tm):
    N, D = x.shape
    row = lambda width: pl.BlockSpec((tm, width), lambda i: (i, 0))
    return pl.pallas_call(
        functools.partial(_outproj_kernel, final_norm=final_norm),
        grid=(N // tm,),
        in_specs=[row(D), row(A_WIDTH), row(B_WIDTH), row(C_WIDTH),
                  _layer_block((D, D), layer, 1),
                  pl.BlockSpec((1, D), lambda i: (0, 0))],
        out_specs=row(D),
        out_shape=jax.ShapeDtypeStruct((N, D), F32),
        compiler_params=_cparams(("arbitrary",)),
        name="outproj",
    )(x, ma, mb, mc, w_all, g_final.reshape(1, D))


def _alibi_slopes():
    return np.exp2(-8.0 * (np.arange(A_HEADS, dtype=np.float64) + 1.0) / A_HEADS).astype(np.float32)


def _prompt_bias():
    slopes = _alibi_slopes()
    qi = np.arange(ATT_BLK)[:, None]
    ki = np.arange(2 * ATT_BLK)[None, :] - ATT_BLK
    dist = qi - ki
    tiles = []
    for w, r in DIL_GROUPS:
        n_back = w // r
        valid = (dist >= 0) & (dist <= n_back)
        pen = (slopes * np.float32(r))[:, None, None] * dist.astype(np.float32)[None]
        b = np.where(valid[None], -pen, np.float32(NEG)).astype(np.float32)
        tiles.append(b.reshape(HEAD_PAIRS, 2 * ATT_BLK, 2 * ATT_BLK))
    normal = np.stack(tiles)
    first = np.concatenate([normal[..., ATT_BLK:], np.full_like(normal[..., ATT_BLK:], NEG)], axis=-1)
    return jnp.asarray(np.stack([normal, first]))


def _attn_unit(q, kwin, vwin, bias, lo_half):
    zero = jnp.zeros_like(q)
    q2 = jnp.concatenate([jnp.where(lo_half, q, zero), jnp.where(lo_half, zero, q)], axis=0)
    s = lax.dot_general(q2, kwin, CONTRACT_LAST, preferred_element_type=F32) + bias
    m = jnp.max(s, axis=-1, keepdims=True)
    p = jnp.exp(s - m)
    den = jnp.sum(p, axis=-1, keepdims=True)
    o = jnp.dot(p.astype(BF16), vwin, preferred_element_type=F32) / den
    lse = m + jnp.log(den)
    o_pair = jnp.where(lo_half, o[:ATT_BLK], o[ATT_BLK:])
    lse_pair = jnp.where(lo_half, lse[:ATT_BLK], lse[ATT_BLK:])
    return o_pair, lse_pair


ATTN_UNROLL = 16


def _attn_prompt_group(g, q_ref, k_ref, v_ref, bias_ref, o_ref, l_ref, part=0, n_parts=1):
    r = DIL_GROUPS[g][1]
    S = o_ref.shape[0]
    nblk = S // r // ATT_BLK
    n_units = (r * nblk) // n_parts
    unroll = min(ATTN_UNROLL, n_units)
    assert n_units * n_parts == r * nblk and n_units % unroll == 0
    lo_half = lax.broadcasted_iota(jnp.int32, (ATT_BLK, LANES), 1) < A_HEAD_DIM

    def trip(it, carry):
        for u in range(unroll):
            n = part * n_units + it * unroll + u
            blk, c = (n, 0) if r == 1 else (lax.div(n, jnp.int32(r)), lax.rem(n, jnp.int32(r)))
            q0 = pl.multiple_of(blk * ATT_BLK, ATT_BLK)
            q = q_ref[c, pl.ds(q0, ATT_BLK), :]
            if nblk == 1:
                kwin, vwin = k_ref[c], v_ref[c]
                bias = bias_ref[0, g, :, ATT_BLK:]
            else:
                k0 = pl.multiple_of(jnp.maximum(blk - 1, 0) * ATT_BLK, ATT_BLK)
                kwin = k_ref[c, pl.ds(k0, 2 * ATT_BLK), :]
                vwin = v_ref[c, pl.ds(k0, 2 * ATT_BLK), :]
                bias = bias_ref[jnp.where(blk == 0, 1, 0), g]
            o_pair, lse_pair = _attn_unit(q, kwin, vwin, bias, lo_half)
            rows = pl.ds(q0, ATT_BLK) if r == 1 else pl.ds(blk * (ATT_BLK * r) + c, ATT_BLK, stride=r)
            o_ref[rows, :] = o_pair
            l_ref[rows, :] = lse_pair
        return carry

    lax.fori_loop(0, n_units // unroll, trip, 0)


def _attn_prompt_merge(o_refs, l_refs, ga_ref, out_ref):
    o1_ref, o4_ref, o16_ref = o_refs
    l1_ref, l4_ref, l16_ref = l_refs
    S = out_ref.shape[0]
    rows_per = 256

    def merge(i, carry):
        rows = pl.ds(pl.multiple_of(i * rows_per, rows_per), rows_per)
        l1, l4, l16 = l1_ref[rows, :], l4_ref[rows, :], l16_ref[rows, :]
        mx = jnp.maximum(jnp.maximum(l1, l4), l16)
        e1, e4, e16 = jnp.exp(l1 - mx), jnp.exp(l4 - mx), jnp.exp(l16 - mx)
        inv = 1.0 / (e1 + e4 + e16)
        y = (e1 * inv) * o1_ref[rows, :] + (e4 * inv) * o4_ref[rows, :] + (e16 * inv) * o16_ref[rows, :]
        out_ref[rows, :] = (y * _silu(ga_ref[rows, :])).astype(BF16)
        return carry

    lax.fori_loop(0, S // rows_per, merge, 0)


def _sample_bias(l_buf, T):
    slopes = _alibi_slopes()
    t = np.arange(T)[:, None]
    rho = np.arange(l_buf)[None, :]
    tn = np.arange(LANES)[None, :]
    neg = np.float32(NEG)
    cache, new = [], []
    for w, r in DIL_GROUPS:
        n_back = w // r
        sl = (slopes * np.float32(r))[:, None, None]
        dc = l_buf + t - rho
        jc = dc // r
        vc = (dc % r == 0) & (jc <= n_back)
        cache.append(np.where(vc[None], -(sl * jc.astype(np.float32)[None]), neg).reshape(A_HEADS * T, l_buf))
        dn = t - tn
        jn = dn // r
        vn = (dn >= 0) & (dn % r == 0) & (jn <= n_back)
        new.append(np.where(vn[None], -(sl * jn.astype(np.float32)[None]), neg).reshape(A_HEADS * T, LANES))
    return (jnp.asarray(np.stack(cache).astype(np.float32)), jnp.asarray(np.stack(new).astype(np.float32)))


def _attn_sample_kernel(q_ref, kn_ref, vn_ref, kc_ref, vc_ref, bc_ref, bn_ref, ga_ref, out_ref):
    T = q_ref.shape[0]
    l_buf = kc_ref.shape[-1]
    lo_half = lax.broadcasted_iota(jnp.int32, (T, LANES), 1) < A_HEAD_DIM
    pad = jnp.zeros((LANES - T, A_WIDTH), F32)
    kn = jnp.concatenate([kn_ref[...], pad], axis=0).astype(BF16)
    vn = jnp.concatenate([vn_ref[...], pad], axis=0).astype(BF16)
    q = q_ref[...]
    sc_parts, sn_parts = [], []
    for hp in range(HEAD_PAIRS):
        lanes = slice(hp * LANES, (hp + 1) * LANES)
        qp = q[:, lanes]
        q2 = jnp.concatenate([jnp.where(lo_half, qp, 0.0), jnp.where(lo_half, 0.0, qp)], axis=0).astype(BF16)
        kt = kc_ref[2 * hp:2 * hp + 2].reshape(LANES, l_buf).astype(BF16)
        sc_parts.append(jnp.dot(q2, kt, preferred_element_type=F32))
        sn_parts.append(lax.dot_general(q2, kn[:, lanes], CONTRACT_LAST, preferred_element_type=F32))
    sc = jnp.concatenate(sc_parts, axis=0)
    sn = jnp.concatenate(sn_parts, axis=0)
    starts = [l_buf - min(w, l_buf) for w, _ in DIL_GROUPS]
    pcs, pns, lses, dens = [], [], [], []
    for g, c0 in enumerate(starts):
        scg = sc[:, c0:] + bc_ref[g, :, c0:]
        sng = sn + bn_ref[g]
        m = jnp.maximum(jnp.max(scg, axis=-1, keepdims=True), jnp.max(sng, axis=-1, keepdims=True))
        pc = jnp.exp(scg - m)
        pn = jnp.exp(sng - m)
        den = jnp.sum(pc, axis=-1, keepdims=True) + jnp.sum(pn, axis=-1, keepdims=True)
        pcs.append(pc)
        pns.append(pn)
        dens.append(den)
        lses.append(m + jnp.log(den))
    mx = jnp.maximum(jnp.maximum(lses[0], lses[1]), lses[2])
    es = [jnp.exp(l - mx) for l in lses]
    tot = es[0] + es[1] + es[2]
    coef = [(e / tot) / d for e, d in zip(es, dens)]
    edges = sorted(set(starts)) + [l_buf]
    segs = []
    for lo, hi in zip(edges[:-1], edges[1:]):
        terms = [coef[g] * pcs[g][:, lo - c0:hi - c0] for g, c0 in enumerate(starts) if c0 <= lo]
        segs.append(functools.reduce(lambda a, b: a + b, terms))
    pc = jnp.concatenate(segs, axis=1).astype(BF16)
    pn = (coef[0] * pns[0] + coef[1] * pns[1] + coef[2] * pns[2]).astype(BF16)
    ys = []
    for hp in range(HEAD_PAIRS):
        lanes = slice(hp * LANES, (hp + 1) * LANES)
        rows = slice(2 * T * hp, 2 * T * (hp + 1))
        vt = vc_ref[2 * hp:2 * hp + 2].reshape(LANES, l_buf).astype(BF16)
        o = lax.dot_general(pc[rows], vt, CONTRACT_LAST, preferred_element_type=F32)
        o = o + jnp.dot(pn[rows], vn[:, lanes], preferred_element_type=F32)
        ys.append(jnp.where(lo_half, o[:T], o[T:]))
    y = jnp.concatenate(ys, axis=1)
    out_ref[...] = (y * _silu(ga_ref[...])).astype(BF16)


N_PHASES = len(DIL_GROUPS) + 1


CACHE_SLOTS = 3
ATTN_VMEM_LIMIT = 56 * 1024 * 1024


def _attn_fused_kernel(*refs, layer):
    n_groups = len(DIL_GROUPS)
    qkv = refs[0:3 * n_groups]
    bias_ref, ga_ref = refs[3 * n_groups:3 * n_groups + 2]
    q_ref, kn_ref, vn_ref, kc_hbm, vc_hbm, bc_ref, bn_ref, gs_ref = refs[3 * n_groups + 2:3 * n_groups + 10]
    out_p_ref, out_s_ref = refs[3 * n_groups + 10:3 * n_groups + 12]
    scratch = refs[3 * n_groups + 12:]
    o_refs, l_refs = scratch[0:n_groups], scratch[n_groups:2 * n_groups]
    kbuf, vbuf, sem = scratch[2 * n_groups:2 * n_groups + 3]
    j = pl.program_id(2)
    n_steps = pl.num_programs(0) * pl.num_programs(1) * pl.num_programs(2)
    n = (pl.program_id(0) * pl.num_programs(1) + pl.program_id(1)) * pl.num_programs(2) + j

    def cache_copies(seq, slot):
        return (pltpu.make_async_copy(kc_hbm.at[layer, seq], kbuf.at[slot], sem.at[0, slot]),
                pltpu.make_async_copy(vc_hbm.at[layer, seq], vbuf.at[slot], sem.at[1, slot]))

    def start(seq):
        for c in cache_copies(seq, lax.rem(seq, CACHE_SLOTS)):
            c.start()

    @pl.when(n == 0)
    def _():
        for seq in range(CACHE_SLOTS - 1):
            start(jnp.int32(seq))

    @pl.when(n + (CACHE_SLOTS - 1) < n_steps)
    def _():
        start(n + (CACHE_SLOTS - 1))

    slot = lax.rem(n, CACHE_SLOTS)
    for c in cache_copies(n, slot):
        c.wait()
    sample_in = (q_ref, kn_ref, vn_ref, kbuf.at[slot], vbuf.at[slot], bc_ref, bn_ref, gs_ref)

    def group(g, part=0, n_parts=1):
        _attn_prompt_group(g, qkv[g], qkv[n_groups + g], qkv[2 * n_groups + g], bias_ref, o_refs[g], l_refs[g],
                           part, n_parts)

    last = n_groups - 1
    for g in range(last):
        @pl.when(j == g)
        def _(g=g):
            group(g)

    @pl.when(j == last)
    def _():
        group(last, 0, 2)

    @pl.when(j == n_groups)
    def _():
        group(last, 1, 2)
        _attn_prompt_merge(o_refs, l_refs, ga_ref, out_p_ref)

    _attn_sample_kernel(*sample_in, out_s_ref)


def _attn_fused(qkv_groups, rest_p, bias_p, q, kn, vn, cache_kt, cache_vt, layer, bias_c, bias_n, rest_s):
    B, S, _ = rest_p.shape
    Bd, T, _ = q.shape
    l_buf = cache_kt.shape[-1]
    HT = A_HEADS * T
    G = len(DIL_GROUPS)
    J = Bd // (B * HEAD_PAIRS)
    assert Bd == B * HEAD_PAIRS * J and J >= N_PHASES and (S // ATT_BLK) % ATTN_UNROLL == 0
    seq = lambda b, hp, j: (b * HEAD_PAIRS + hp) * J + j
    ins = list(qkv_groups)
    specs = [pl.BlockSpec((None, None) + t.shape[2:], lambda b, hp, j: (b, hp, 0, 0, 0)) for t in ins]
    tok = pl.BlockSpec((None, T, A_WIDTH), lambda b, hp, j: (seq(b, hp, j), 0, 0))
    cache = pl.BlockSpec(memory_space=pl.ANY)
    cache_slots = pltpu.VMEM((CACHE_SLOTS, A_HEADS, A_HEAD_DIM, l_buf), cache_kt.dtype)
    ins += [bias_p, rest_p, q, kn, vn, cache_kt, cache_vt, bias_c, bias_n, rest_s]
    specs += [pl.BlockSpec((2, G, None, 2 * ATT_BLK, 2 * ATT_BLK), lambda b, hp, j: (0, 0, hp, 0, 0)),
              pl.BlockSpec((None, S, LANES), lambda b, hp, j: (b, 0, REST_GA + hp)),
              tok, tok, tok, cache, cache,
              pl.BlockSpec((G, HT, l_buf), lambda b, hp, j: (0, 0, 0)),
              pl.BlockSpec((G, HT, LANES), lambda b, hp, j: (0, 0, 0)),
              pl.BlockSpec((None, T, A_WIDTH), lambda b, hp, j: (seq(b, hp, j), 0, REST_GA))]
    return pl.pallas_call(
        functools.partial(_attn_fused_kernel, layer=layer),
        grid=(B, HEAD_PAIRS, J),
        in_specs=specs,
        out_specs=[pl.BlockSpec((None, S, LANES), lambda b, hp, j: (b, 0, hp)), tok],
        out_shape=[jax.ShapeDtypeStruct((B, S, A_WIDTH), BF16),
                   jax.ShapeDtypeStruct((Bd, T, A_WIDTH), BF16)],
        scratch_shapes=[pltpu.VMEM((S, LANES), F32)] * (2 * G)
                       + [cache_slots, cache_slots, pltpu.SemaphoreType.DMA((2, CACHE_SLOTS))],
        compiler_params=pltpu.CompilerParams(dimension_semantics=("arbitrary", "arbitrary", "arbitrary"),
                                             vmem_limit_bytes=ATTN_VMEM_LIMIT),
        name="attn_fused",
    )(*ins)


def _retention_consts(chunk, nseq):
    H = B_HEADS
    log_g = np.log(1.0 - np.exp2(-5.0 - np.arange(H, dtype=np.float64)))
    i = np.arange(chunk, dtype=np.float64)
    diff = i[:, None] - i[None, :]
    dmat = np.where(diff >= 0, np.exp(np.maximum(diff, 0.0)[None] * log_g[:, None, None]), 0.0)
    q_dec = np.exp((i[:, None] + 1.0) * log_g[None, :])
    k_dec = np.exp((chunk - 1.0 - i)[:, None] * log_g[None, :])
    c_dec = np.exp(chunk * log_g)
    rows = nseq * chunk
    eye = np.eye(nseq)
    dmat = (eye[None, :, None, :, None] * dmat[:, None, :, None, :]).reshape(H, rows, rows)
    lanes = lambda t: np.repeat(t, B_DIM, axis=-1)
    pair = lambda t: t.reshape(t.shape[0], RET_PAIRS, LANES).transpose(1, 0, 2)
    dmat2 = dmat.reshape(RET_PAIRS, 2 * rows, rows)
    qd = pair(lanes(np.tile(q_dec, (nseq, 1))))
    kd = pair(lanes(np.tile(k_dec, (nseq, 1))))
    cd = lanes(c_dec[None, :]).reshape(RET_PAIRS, 1, LANES)
    cd = np.broadcast_to(cd, (RET_PAIRS, LANES, LANES))
    return tuple(jnp.asarray(np.ascontiguousarray(a).astype(np.float32)) for a in (dmat2, qd, kd, cd))


def _ret_intra(qf, kf, v, dmat, lo_half):
    rows = qf.shape[0]
    q2 = jnp.concatenate([jnp.where(lo_half, qf, 0.0), jnp.where(lo_half, 0.0, qf)], axis=0).astype(BF16)
    att = lax.dot_general(q2, kf.astype(BF16), CONTRACT_LAST, preferred_element_type=F32) * dmat
    oi = jnp.dot(att.astype(BF16), v, preferred_element_type=F32)
    return jnp.where(lo_half, oi[:rows], oi[rows:])


def _ret_norm_gate(o, gn, gate, lo_half):
    def head_sum(x):
        a = jnp.sum(jnp.where(lo_half, x, 0.0), axis=-1, keepdims=True)
        b = jnp.sum(jnp.where(lo_half, 0.0, x), axis=-1, keepdims=True)
        return jnp.where(lo_half, a, b)

    mu = head_sum(o) / B_DIM
    d = o - mu
    var = head_sum(d * d) / B_DIM
    y = d * lax.rsqrt(var + EPS) * gn
    return (y * _silu(gate)).astype(BF16)


def _same_head_mask():
    r2 = lax.broadcasted_iota(jnp.int32, (LANES, LANES), 0) < B_DIM
    c2 = lax.broadcasted_iota(jnp.int32, (LANES, LANES), 1) < B_DIM
    return r2 == c2


RET_UNROLL = 4


def _retention_prompt_kernel(q_ref, k_ref, v_ref, gate_ref, s0_ref, dmat_ref, qd_ref, kd_ref, cd_ref, gn_ref,
                             out_ref, sfin_ref, oi_ref, kv_ref, st_ref, *, chunk):
    L = q_ref.shape[0]
    nc = L // chunk
    lo_half = lax.broadcasted_iota(jnp.int32, (chunk, LANES), 1) < B_DIM
    same_head = _same_head_mask()

    def chunk_rows(ci):
        return pl.ds(pl.multiple_of(ci * chunk, chunk), chunk)

    def phase1(it, carry):
        for u in range(RET_UNROLL):
            ci = it * RET_UNROLL + u
            rows = chunk_rows(ci)
            qf = q_ref[rows, :]
            kf = k_ref[rows, :] * (B_DIM ** -0.5)
            v = v_ref[rows, :].astype(BF16)
            oi_ref[rows, :] = _ret_intra(qf, kf, v, dmat_ref[...], lo_half)
            kv = lax.dot_general((kf * kd_ref[...]).astype(BF16), v, CONTRACT_FIRST, preferred_element_type=F32)
            kv_ref[ci] = jnp.where(same_head, kv, 0.0)
        return carry

    lax.fori_loop(0, nc // RET_UNROLL, phase1, 0)

    def phase2(ci, state):
        st_ref[ci] = state.astype(BF16)
        return state * cd_ref[...] + kv_ref[ci]

    sfin_ref[...] = lax.fori_loop(0, nc, phase2, s0_ref[...])

    def phase3(it, carry):
        for u in range(RET_UNROLL):
            ci = it * RET_UNROLL + u
            rows = chunk_rows(ci)
            inter = jnp.dot(q_ref[rows, :].astype(BF16), st_ref[ci], preferred_element_type=F32) * qd_ref[...]
            out_ref[rows, :] = _ret_norm_gate(oi_ref[rows, :] + inter, gn_ref[...], gate_ref[rows, :], lo_half)
        return carry

    lax.fori_loop(0, nc // RET_UNROLL, phase3, 0)


def _retention_prompt(rest, s0_bd, consts, gn_all, layer, chunk):
    B, L, _ = rest.shape
    assert (L // chunk) % RET_UNROLL == 0
    dmat2, qd, kd, cd = consts
    col = lambda blk: pl.BlockSpec((None, L, LANES), lambda b, p, blk=blk: (b, 0, blk + p))
    pair_const = lambda a: pl.BlockSpec((None,) + a.shape[1:], lambda b, p: (p,) + (0,) * (a.ndim - 1))
    st = pl.BlockSpec((None, None, LANES, LANES), lambda b, p: (b, p, 0, 0))
    gn_spec = pl.BlockSpec((None, None, 1, LANES), lambda b, p: (layer, p, 0, 0))
    return pl.pallas_call(
        functools.partial(_retention_prompt_kernel, chunk=chunk),
        grid=(B, RET_PAIRS),
        in_specs=[col(REST_QB), col(REST_KB), col(REST_VB), col(REST_GB), st,
                  pair_const(dmat2), pair_const(qd), pair_const(kd), pair_const(cd), gn_spec],
        out_specs=[pl.BlockSpec((None, L, LANES), lambda b, p: (b, 0, p)), st],
        out_shape=[jax.ShapeDtypeStruct((B, L, B_WIDTH), BF16),
                   jax.ShapeDtypeStruct((B, RET_PAIRS, LANES, LANES), F32)],
        scratch_shapes=[pltpu.VMEM((L, LANES), F32),
                        pltpu.VMEM((L // chunk, LANES, LANES), F32),
                        pltpu.VMEM((L // chunk, LANES, LANES), BF16)],
        compiler_params=_cparams(("arbitrary", "arbitrary")),
        name="retention_prompt",
    )(rest, rest, rest, rest, s0_bd, dmat2, qd, kd, cd, gn_all)


def _retention_sample_kernel(q_ref, k_ref, v_ref, gate_ref, s0_ref, dmat_ref, qd_ref, kd_ref, cd_ref, gn_ref,
                             out_ref, sfin_ref, *, T):
    rows = q_ref.shape[0]
    nseq = rows // T
    lo_half = lax.broadcasted_iota(jnp.int32, (rows, LANES), 1) < B_DIM
    seq_of_row = lax.broadcasted_iota(jnp.int32, (rows, LANES), 0) // T

    def expand(x):
        return jnp.concatenate([jnp.where(seq_of_row == s, x, 0.0) for s in range(nseq)], axis=1).astype(BF16)

    qf = q_ref[...]
    kf = k_ref[...] * (B_DIM ** -0.5)
    v = v_ref[...].astype(BF16)
    s0 = s0_ref[...]
    o = _ret_intra(qf, kf, v, dmat_ref[...], lo_half)
    s0_rows = s0.reshape(nseq * LANES, LANES).astype(BF16)
    o = o + jnp.dot(expand(qf), s0_rows, preferred_element_type=F32) * qd_ref[...]
    kv = lax.dot_general(expand(kf * kd_ref[...]), v, CONTRACT_FIRST, preferred_element_type=F32)
    out_ref[...] = _ret_norm_gate(o, gn_ref[...], gate_ref[...], lo_half)
    sfin_ref[...] = s0 * cd_ref[...][None] + jnp.where(_same_head_mask()[None],
                                                      kv.reshape(nseq, LANES, LANES), 0.0)


def _retention_sample(rest, s0_bd, consts, gn_all, layer, T, nseq=16):
    N, _ = rest.shape
    rows = nseq * T
    dmat2, qd, kd, cd = consts
    col = lambda blk: pl.BlockSpec((rows, LANES), lambda i, p, blk=blk: (i, blk + p))
    pair_const = lambda a: pl.BlockSpec((None,) + a.shape[1:], lambda i, p: (p,) + (0,) * (a.ndim - 1))
    st = pl.BlockSpec((nseq, None, LANES, LANES), lambda i, p: (i, p, 0, 0))
    gn_spec = pl.BlockSpec((None, None, 1, LANES), lambda i, p: (layer, p, 0, 0))
    return pl.pallas_call(
        functools.partial(_retention_sample_kernel, T=T),
        grid=(N // rows, RET_PAIRS),
        in_specs=[col(REST_QB), col(REST_KB), col(REST_VB), col(REST_GB), st,
                  pair_const(dmat2), pair_const(qd), pair_const(kd), pair_const(cd), gn_spec],
        out_specs=[pl.BlockSpec((rows, LANES), lambda i, p: (i, p)), st],
        out_shape=[jax.ShapeDtypeStruct((N, B_WIDTH), BF16),
                   jax.ShapeDtypeStruct(s0_bd.shape, F32)],
        compiler_params=_cparams(("arbitrary", "arbitrary")),
        name="retention_sample",
    )(rest, rest, rest, rest, s0_bd, dmat2, qd, kd, cd, gn_all)


def _state_to_pairs(s):
    B = s.shape[0]
    s = s.reshape(B, RET_PAIRS, 2, B_DIM, B_DIM)
    z = jnp.zeros_like(s[:, :, 0])
    top = jnp.concatenate([s[:, :, 0], z], axis=-1)
    bot = jnp.concatenate([z, s[:, :, 1]], axis=-1)
    return jnp.concatenate([top, bot], axis=-2)


def _pairs_to_state(sbd):
    B = sbd.shape[0]
    a = sbd[:, :, :B_DIM, :B_DIM]
    b = sbd[:, :, B_DIM:, B_DIM:]
    return jnp.stack([a, b], axis=2).reshape(B, B_HEADS, B_DIM, B_DIM)


CONV_STATE = CONV_W - 1
CONV_PAD = 32
LANE_HALVES = C_WIDTH // LANES


def _conv_param_specs(layer, grid_rank):
    vec = _layer_block((1, C_WIDTH), layer, grid_rank)
    return [_layer_block((CONV_W, C_WIDTH), layer, grid_rank), vec, vec, vec,
            _layer_block((C_WIDTH, C_WIDTH), layer, grid_rank), vec]


def _conv_kernel(ua_ref, ub_ref, gc_ref, s0_ref, dww_ref, dwb_ref, lng_ref, lnb_ref, pww_ref, pwb_ref,
                 out_ref, sfin_ref, ext_ref, *, tile):
    L = ua_ref.shape[0]
    u = ua_ref[...] * jax.nn.sigmoid(ub_ref[...])
    s0 = s0_ref[...]
    for h in range(LANE_HALVES):
        lanes = slice(h * LANES, (h + 1) * LANES)
        ext_ref[h, 0:CONV_PAD, :] = s0[:, lanes]
        ext_ref[h, CONV_PAD:CONV_PAD + L, :] = u[:, lanes]
    for h in range(LANE_HALVES):
        sfin_ref[:, h * LANES:(h + 1) * LANES] = ext_ref[h, L:L + CONV_PAD, :]

    def step(i, carry):
        t0 = pl.multiple_of(i * tile, tile)
        halves = []
        for h in range(LANE_HALVES):
            acc = jnp.zeros((tile, LANES), F32)
            for w in range(CONV_W):
                rows = pl.ds(t0 + (CONV_PAD - CONV_STATE + w), tile, stride=1)
                acc = acc + ext_ref[h, rows, :] * dww_ref[w:w + 1, h * LANES:(h + 1) * LANES]
            halves.append(acc)
        y = jnp.concatenate(halves, axis=1) + dwb_ref[...]
        mu = jnp.mean(y, axis=-1, keepdims=True)
        d = y - mu
        var = jnp.mean(d * d, axis=-1, keepdims=True)
        yn = d * lax.rsqrt(var + EPS) * lng_ref[...] + lnb_ref[...]
        yn = _silu(yn).astype(BF16)
        z = jnp.dot(yn, pww_ref[...], preferred_element_type=F32) + pwb_ref[...]
        rows = pl.ds(t0, tile)
        out_ref[rows, :] = (z * _silu(gc_ref[rows, :])).astype(BF16)
        return carry

    lax.fori_loop(0, L // tile, step, 0)


def _conv(rest, s0, conv_params, layer, tile):
    B, L, _ = rest.shape
    col = lambda blk: pl.BlockSpec((None, L, C_WIDTH), lambda b, blk=blk: (b, 0, blk))
    st = pl.BlockSpec((None, CONV_PAD, C_WIDTH), lambda b: (b, 0, 0))
    s0 = jnp.pad(s0, ((0, 0), (CONV_PAD - CONV_STATE, 0), (0, 0)))
    out, sfin = pl.pallas_call(
        functools.partial(_conv_kernel, tile=tile),
        grid=(B,),
        in_specs=[col(REST_UA), col(REST_UB), col(REST_GC), st] + _conv_param_specs(layer, 1),
        out_specs=[pl.BlockSpec((None, L, C_WIDTH), lambda b: (b, 0, 0)), st],
        out_shape=[jax.ShapeDtypeStruct((B, L, C_WIDTH), BF16),
                   jax.ShapeDtypeStruct((B, CONV_PAD, C_WIDTH), F32)],
        scratch_shapes=[pltpu.VMEM((LANE_HALVES, CONV_PAD + L, LANES), F32)],
        compiler_params=_cparams(("arbitrary",)),
        name="conformer_conv",
    )(rest, rest, rest, s0, *conv_params)
    return out, sfin[:, CONV_PAD - CONV_STATE:]


def _conv_sample_kernel(ua_ref, ub_ref, gc_ref, s0_ref, dww_ref, dwb_ref, lng_ref, lnb_ref, pww_ref, pwb_ref,
                        out_ref, sfin_ref, u_ref, ut_ref, z_ref, *, T):
    N = ua_ref.shape[0]
    Bd = N // T
    u = ua_ref[...] * jax.nn.sigmoid(ub_ref[...])
    for h in range(LANE_HALVES):
        u_ref[h] = u[:, h * LANES:(h + 1) * LANES]
    for t in range(T):
        ut_ref[t] = jnp.concatenate([u_ref[h, pl.ds(t, Bd, stride=T), :] for h in range(LANE_HALVES)], axis=1)

    def ext(tau):
        return s0_ref[tau] if tau < CONV_STATE else ut_ref[tau - CONV_STATE]

    for tau in range(CONV_STATE):
        sfin_ref[tau] = ext(tau + T)
    for t in range(T):
        acc = jnp.zeros((Bd, C_WIDTH), F32)
        for w in range(CONV_W):
            acc = acc + ext(t + w) * dww_ref[w:w + 1, :]
        y = acc + dwb_ref[...]
        mu = jnp.mean(y, axis=-1, keepdims=True)
        d = y - mu
        var = jnp.mean(d * d, axis=-1, keepdims=True)
        yn = d * lax.rsqrt(var + EPS) * lng_ref[...] + lnb_ref[...]
        yn = _silu(yn).astype(BF16)
        z = jnp.dot(yn, pww_ref[...], preferred_element_type=F32) + pwb_ref[...]
        for h in range(LANE_HALVES):
            z_ref[h, pl.ds(t, Bd, stride=T), :] = z[:, h * LANES:(h + 1) * LANES]
    z = jnp.concatenate([z_ref[h] for h in range(LANE_HALVES)], axis=1)
    out_ref[...] = (z * _silu(gc_ref[...])).astype(BF16)


def _conv_sample(rest, state_t, conv_params, layer, T):
    N, _ = rest.shape
    Bd = N // T
    col = lambda blk: pl.BlockSpec((N, C_WIDTH), lambda i, blk=blk: (0, blk))
    return pl.pallas_call(
        functools.partial(_conv_sample_kernel, T=T),
        grid=(1,),
        in_specs=[col(REST_UA), col(REST_UB), col(REST_GC),
                  pl.BlockSpec((None, CONV_STATE, Bd, C_WIDTH), lambda i: (layer, 0, 0, 0))]
                 + _conv_param_specs(layer, 1),
        out_specs=[pl.BlockSpec((N, C_WIDTH), lambda i: (0, 0)),
                   pl.BlockSpec((CONV_STATE, Bd, C_WIDTH), lambda i: (0, 0, 0))],
        out_shape=[jax.ShapeDtypeStruct((N, C_WIDTH), BF16),
                   jax.ShapeDtypeStruct((CONV_STATE, Bd, C_WIDTH), F32)],
        scratch_shapes=[pltpu.VMEM((LANE_HALVES, N, LANES), F32),
                        pltpu.VMEM((T, Bd, C_WIDTH), F32),
                        pltpu.VMEM((LANE_HALVES, N, LANES), F32)],
        compiler_params=_cparams(("arbitrary",)),
        name="conformer_conv_sample",
    )(rest, rest, rest, state_t, *conv_params)


def kernel(x_prompt, x_sample, cache_attn_k, cache_attn_v, state_retention, state_conv, norm_g, w_in, w_out,
           ret_norm_g, conv_dw_w, conv_dw_b, conv_ln_g, conv_ln_b, conv_pw_w, conv_pw_b, final_norm_g):
    Bp, S, D = x_prompt.shape
    Bd, T, _ = x_sample.shape
    depth = w_in.shape[0]
    l_buf = cache_attn_k.shape[2]
    assert S % (ATT_BLK * DIL_GROUPS[-1][1]) == 0 and S % RET_CHUNK == 0 and l_buf == WIN_MAX
    lbuf_p = min(WIN_MAX, S)
    ret_nseq = 128 // T
    ck = jnp.transpose(cache_attn_k, (0, 1, 3, 4, 2))
    cv = jnp.transpose(cache_attn_v, (0, 1, 3, 4, 2))
    conv_state_t = jnp.transpose(state_conv, (0, 2, 1, 3))

    w_in_b = w_in.astype(BF16)
    w_out_b = w_out.astype(BF16)
    norm_g3 = norm_g.reshape(depth, 1, D)
    ret_gn = ret_norm_g.reshape(depth, RET_PAIRS, 1, LANES)
    vec3 = lambda a: a.reshape(depth, 1, C_WIDTH)
    conv_params = (conv_dw_w, vec3(conv_dw_b), vec3(conv_ln_g), vec3(conv_ln_b), conv_pw_w.astype(BF16),
                   vec3(conv_pw_b))
    bias_p = _prompt_bias()
    bias_c, bias_n = _sample_bias(l_buf, T)
    ret_consts_p = _retention_consts(RET_CHUNK, 1)
    ret_consts_s = _retention_consts(T, ret_nseq)
    zero_ret = jnp.zeros((Bp, RET_PAIRS, LANES, LANES), F32)
    zero_conv = jnp.zeros((Bp, CONV_STATE, C_WIDTH), F32)

    xp = x_prompt
    xs = x_sample.reshape(Bd * T, D)
    kp_l, vp_l, ks_l, vs_l, rp_l, rs_l, cp_l, cs_l = [], [], [], [], [], [], [], []
    for l in range(depth):
        last = l == depth - 1
        qkv_groups, kf, vf, rest = _inproj_prompt(xp, norm_g3, w_in_b, l)
        q_s, kf_s, vf_s, rest_s = _inproj_sample(xs, norm_g3, w_in_b, l)
        tok = lambda a: a.reshape(Bd, T, -1)
        ma, ma_s = _attn_fused(qkv_groups, rest, bias_p, tok(q_s), tok(kf_s), tok(vf_s), ck, cv, l,
                               bias_c, bias_n, tok(rest_s))
        mb, rp = _retention_prompt(rest, zero_ret, ret_consts_p, ret_gn, l, RET_CHUNK)
        mc, cp = _conv(rest, zero_conv, conv_params, l, tile=512)
        xp = _outproj(xp.reshape(Bp * S, D), ma.reshape(Bp * S, -1), mb.reshape(Bp * S, -1),
                      mc.reshape(Bp * S, -1), w_out_b, l, final_norm_g, last, tm=512).reshape(Bp, S, D)
        kp_l.append(kf[:, S - lbuf_p:].reshape(Bp, lbuf_p, A_HEADS, A_HEAD_DIM))
        vp_l.append(vf[:, S - lbuf_p:].reshape(Bp, lbuf_p, A_HEADS, A_HEAD_DIM))
        rp_l.append(_pairs_to_state(rp))
        cp_l.append(cp)
        mb, rs = _retention_sample(rest_s, _state_to_pairs(state_retention[l]), ret_consts_s, ret_gn, l,
                                   T, ret_nseq)
        mc, cs = _conv_sample(rest_s, conv_state_t, conv_params, l, T=T)
        xs = _outproj(xs, ma_s.reshape(Bd * T, -1), mb, mc, w_out_b, l, final_norm_g, last, tm=256)
        ks_l.append(kf_s.reshape(Bd, T, A_HEADS, A_HEAD_DIM))
        vs_l.append(vf_s.reshape(Bd, T, A_HEADS, A_HEAD_DIM))
        rs_l.append(_pairs_to_state(rs))
        cs_l.append(cs)
    return (xp, xs.reshape(Bd, T, D),
            jnp.stack(kp_l), jnp.stack(vp_l), jnp.stack(ks_l), jnp.stack(vs_l),
            jnp.stack(rp_l), jnp.stack(rs_l), jnp.stack(cp_l), jnp.transpose(jnp.stack(cs_l), (0, 2, 1, 3)))
```
